```python
import jax, jax.numpy as jnp
from jax import lax
import numpy as np

D_MODEL = 1024
BATCH = 2
SEQ = 8192
DEPTH = 1

N_HEADS = 8
QK_NOPE_DIM = 64
QK_ROPE_DIM = 32
QK_DIM = QK_NOPE_DIM + QK_ROPE_DIM
V_DIM = 64
Q_LORA = 256
KV_LORA = 256
ROPE_THETA = 10000.0
Q_BLOCK = 128
F_GROUPS = 4
F_GROUP_DIM = 128
F_DIM = F_GROUPS * F_GROUP_DIM
N_BRANCH = 2
IN_DIM = Q_LORA + KV_LORA + QK_ROPE_DIM + F_DIM + N_BRANCH * D_MODEL
SPLITS = (Q_LORA, Q_LORA + KV_LORA, Q_LORA + KV_LORA + QK_ROPE_DIM, Q_LORA + KV_LORA + QK_ROPE_DIM + F_DIM)
PEER_HEADS = 8
PEER_NKEYS = 128
PEER_EXPERTS = PEER_NKEYS * PEER_NKEYS
PEER_QDIM = 256
PEER_HALF = PEER_QDIM // 2
PEER_TOPK = 16
PEER_CHUNK = 128
DN_ALPHA = (2.0 * DEPTH) ** 0.25
DN_BETA = (8.0 * DEPTH) ** -0.25
LN_EPS = 1e-5
RMS_EPS = 1e-6

kernel_name = "hybrid_mla_fnet_peer_deepnorm_encoder"


def layer_norm(x, g, b):
    xf = x.astype(jnp.float32)
    mu = jnp.mean(xf, axis=-1, keepdims=True)
    var = jnp.mean(jnp.square(xf - mu), axis=-1, keepdims=True)
    y = (xf - mu) * lax.rsqrt(var + LN_EPS) * g.astype(jnp.float32) + b.astype(jnp.float32)
    return y.astype(x.dtype)


def rms_norm(x, g):
    xf = x.astype(jnp.float32)
    y = xf * lax.rsqrt(jnp.mean(jnp.square(xf), axis=-1, keepdims=True) + RMS_EPS) * g.astype(jnp.float32)
    return y.astype(x.dtype)


def rotary_tables(seq_len):
    pos = jnp.arange(seq_len, dtype=jnp.float32)
    inv = 1.0 / (ROPE_THETA ** (jnp.arange(0, QK_ROPE_DIM, 2, dtype=jnp.float32) / QK_ROPE_DIM))
    ang = pos[:, None] * inv[None, :]
    return jnp.cos(ang), jnp.sin(ang)


def apply_rope(x, cos, sin):
    xf = x.astype(jnp.float32)
    half = QK_ROPE_DIM // 2
    x1, x2 = xf[..., :half], xf[..., half:]
    return jnp.concatenate([x1 * cos - x2 * sin, x2 * cos + x1 * sin], axis=-1).astype(x.dtype)


def mla_branch(c_q, c_kv, k_rope_raw, q_norm_g, kv_norm_g, w_uq, w_uk, w_uv, w_o_attn, cos, sin):
    B, S, _ = c_q.shape
    q = (rms_norm(c_q, q_norm_g) @ w_uq).reshape(B, S, N_HEADS, QK_DIM) * (QK_DIM ** -0.5)
    q_nope = q[..., :QK_NOPE_DIM]
    q_rope = apply_rope(q[..., QK_NOPE_DIM:], cos[None, :, None, :], sin[None, :, None, :])
    c = rms_norm(c_kv, kv_norm_g)
    k_nope = (c @ w_uk).reshape(B, S, N_HEADS, QK_NOPE_DIM)
    v = (c @ w_uv).reshape(B, S, N_HEADS, V_DIM)
    k_rope = apply_rope(k_rope_raw, cos[None], sin[None])
    nb = S // Q_BLOCK
    qn_blocks = q_nope.reshape(B, nb, Q_BLOCK, N_HEADS, QK_NOPE_DIM).transpose(1, 0, 2, 3, 4)
    qr_blocks = q_rope.reshape(B, nb, Q_BLOCK, N_HEADS, QK_ROPE_DIM).transpose(1, 0, 2, 3, 4)

    def attend(blk):
        qn, qr = blk
        s = (jnp.einsum('bqhd,bkhd->bhqk', qn, k_nope, preferred_element_type=jnp.float32)
             + jnp.einsum('bqhr,bkr->bhqk', qr, k_rope, preferred_element_type=jnp.float32))
        p = jax.nn.softmax(s, axis=-1).astype(v.dtype)
        return jnp.einsum('bhqk,bkhd->bqhd', p, v)

    o = lax.map(attend, (qn_blocks, qr_blocks))
    o = o.transpose(1, 0, 2, 3, 4).reshape(B, S, N_HEADS * V_DIM)
    return o @ w_o_attn


def fourier_branch(z, w_fourier):
    B, S, _ = z.shape
    zf = z.astype(jnp.float32).reshape(B, S, F_GROUPS, F_GROUP_DIM)
    y = jnp.fft.fft2(zf, axes=(1, 3), norm='ortho').real
    return y.astype(z.dtype).reshape(B, S, F_DIM) @ w_fourier


def peer(x, w_q, keys, w_u, w_v):
    B, S, D = x.shape
    T = B * S
    xt = x.reshape(T, D)
    q = (xt @ w_q).reshape(T, PEER_HEADS, 2, PEER_HALF)
    s_half = jnp.einsum('thcd,hcnd->thcn', q, keys, preferred_element_type=jnp.float32)
    sv, si = lax.top_k(s_half, PEER_TOPK)
    cand = (sv[:, :, 0, :, None] + sv[:, :, 1, None, :]).reshape(T, PEER_HEADS, PEER_TOPK * PEER_TOPK)
    cv, ci = lax.top_k(cand, PEER_TOPK)
    ia = jnp.take_along_axis(si[:, :, 0], ci // PEER_TOPK, axis=-1)
    ib = jnp.take_along_axis(si[:, :, 1], ci % PEER_TOPK, axis=-1)
    expert = (ia * PEER_NKEYS + ib).reshape(T, PEER_HEADS * PEER_TOPK)
    gate = jax.nn.softmax(cv, axis=-1).reshape(T, PEER_HEADS * PEER_TOPK)
    nc = T // PEER_CHUNK
    hk = PEER_HEADS * PEER_TOPK

    def run(blk):
        xc, ec, gc = blk
        u = jnp.take(w_u, ec, axis=0)
        h = jnp.einsum('cd,ced->ce', xc, u, preferred_element_type=jnp.float32)
        a = (jax.nn.gelu(h, approximate=False) * gc).astype(x.dtype)
        return jnp.einsum('ce,ced->cd', a, jnp.take(w_v, ec, axis=0))

    y = lax.map(run, (xt.reshape(nc, PEER_CHUNK, D), expert.reshape(nc, PEER_CHUNK, hk), gate.reshape(nc, PEER_CHUNK, hk)))
    return y.reshape(B, S, D)


def setup_inputs(seed: int = 0) -> dict:
    key = jax.random.key(seed)
    ks = jax.random.split(key, 24)
    f32 = jnp.float32

    def nrm(k, shape, scale):
        return jax.random.normal(k, shape, f32) * scale

    L = DEPTH
    return {
        'x': nrm(ks[0], (BATCH, SEQ, D_MODEL), 1.0),
        'ln0_g': 1.0 + nrm(ks[1], (D_MODEL,), 0.02),
        'ln0_b': nrm(ks[2], (D_MODEL,), 0.02),
        'w_in': nrm(ks[3], (L, D_MODEL, IN_DIM), D_MODEL ** -0.5),
        'b_in': nrm(ks[4], (L, IN_DIM), 0.02),
        'q_norm_g': 1.0 + nrm(ks[5], (L, Q_LORA), 0.02),
        'kv_norm_g': 1.0 + nrm(ks[6], (L, KV_LORA), 0.02),
        'w_uq': nrm(ks[7], (L, Q_LORA, N_HEADS * QK_DIM), Q_LORA ** -0.5),
        'w_uk': nrm(ks[8], (L, KV_LORA, N_HEADS * QK_NOPE_DIM), KV_LORA ** -0.5),
        'w_uv': nrm(ks[9], (L, KV_LORA, N_HEADS * V_DIM), DN_BETA * KV_LORA ** -0.5),
        'w_o_attn': nrm(ks[10], (L, N_HEADS * V_DIM, D_MODEL), (N_HEADS * V_DIM) ** -0.5),
        'w_fourier': nrm(ks[11], (L, F_DIM, D_MODEL), DN_BETA * F_DIM ** -0.5),
        'w_out': nrm(ks[12], (L, D_MODEL, D_MODEL), DN_BETA * D_MODEL ** -0.5),
        'b_out': nrm(ks[13], (L, D_MODEL), 0.02),
        'ln1_g': 1.0 + nrm(ks[14], (L, D_MODEL), 0.02),
        'ln1_b': nrm(ks[15], (L, D_MODEL), 0.02),
        'peer_w_q': nrm(ks[16], (L, D_MODEL, PEER_HEADS * PEER_QDIM), D_MODEL ** -0.5),
        'peer_keys': nrm(ks[17], (L, PEER_HEADS, 2, PEER_NKEYS, PEER_HALF), PEER_HALF ** -0.5),
        'peer_w_u': nrm(ks[18], (L, PEER_EXPERTS, D_MODEL), D_MODEL ** -0.5),
        'peer_w_v': nrm(ks[19], (L, PEER_EXPERTS, D_MODEL), DN_BETA),
        'ln2_g': 1.0 + nrm(ks[20], (L, D_MODEL), 0.02),
        'ln2_b': nrm(ks[21], (L, D_MODEL), 0.02),
    }


def reference(x, ln0_g, ln0_b, w_in, b_in, q_norm_g, kv_norm_g, w_uq, w_uk, w_uv, w_o_attn, w_fourier,
              w_out, b_out, ln1_g, ln1_b, peer_w_q, peer_keys, peer_w_u, peer_w_v, ln2_g, ln2_b):
    B, S, D = x.shape
    cos, sin = rotary_tables(S)
    h = layer_norm(x, ln0_g, ln0_b)
    for l in range(DEPTH):
        p = h @ w_in[l] + b_in[l]
        c_q, c_kv, k_r, z_f, g_logits = jnp.split(p, SPLITS, axis=-1)
        y_a = mla_branch(c_q, c_kv, k_r, q_norm_g[l], kv_norm_g[l], w_uq[l], w_uk[l], w_uv[l], w_o_attn[l], cos, sin)
        y_f = fourier_branch(z_f, w_fourier[l])
        g = jax.nn.sigmoid(g_logits.astype(jnp.float32)).astype(h.dtype).reshape(B, S, N_BRANCH, D)
        m = g[:, :, 0, :] * y_a + g[:, :, 1, :] * y_f
        h = layer_norm(DN_ALPHA * h + (m @ w_out[l] + b_out[l]), ln1_g[l], ln1_b[l])
        h = layer_norm(DN_ALPHA * h + peer(h, peer_w_q[l], peer_keys[l], peer_w_u[l], peer_w_v[l]), ln2_g[l], ln2_b[l])
    return h
```

```python
import functools
import math

import jax
import jax.numpy as jnp
from jax import lax
from jax.experimental import pallas as pl
from jax.experimental.pallas import tpu as pltpu

F32 = jnp.float32
BF16 = jnp.bfloat16

D_MODEL = 1024
N_HEADS = 8
QK_NOPE_DIM = 64
QK_ROPE_DIM = 32
ROPE_HALF = QK_ROPE_DIM // 2
QK_DIM = QK_NOPE_DIM + QK_ROPE_DIM
V_DIM = 64
Q_LORA = 256
KV_LORA = 256
ROPE_THETA = 10000.0
F_GROUPS = 4
F_GROUP_DIM = 128
F_DIM = F_GROUPS * F_GROUP_DIM
PEER_HEADS = 8
PEER_NKEYS = 128
PEER_EXPERTS = PEER_NKEYS * PEER_NKEYS
PEER_HALF = 128
PEER_TOPK = 16
DEPTH = 1
DN_ALPHA = (2.0 * DEPTH) ** 0.25
LN_EPS = 1e-5
RMS_EPS = 1e-6

LANES = 128
SUBLANES = 8
HEAD_PAD = LANES
VMEM_LIMIT_BYTES = 56 * 1024 * 1024

TM_LN = 512
TM_PREP = 256
TQ_ATTN = 256
TK_ATTN = 512
TM_FG = 256
FFT_A_LANES = 4096
FFT_B_K1 = 8
TM_POST = 256
TM_TOPK = 256
TM_PEER = 512
EB_PEER = 512

_CAND_GROUPS = ((0, 0, 8), (0, 8, 8), (1, 0, 8), (2, 0, 5), (3, 0, 4), (4, 0, 3), (5, 0, 2), (6, 0, 2), (7, 0, 2))


def _params(sem):
    return pltpu.CompilerParams(dimension_semantics=sem, vmem_limit_bytes=VMEM_LIMIT_BYTES)


def _const_spec(shape):
    nd = len(shape)
    return pl.BlockSpec(shape, lambda *_: (0,) * nd)


def _layer_norm(x, g, b):
    mu = jnp.mean(x, axis=-1, keepdims=True)
    xc = x - mu
    var = jnp.mean(xc * xc, axis=-1, keepdims=True)
    return xc * lax.rsqrt(var + LN_EPS) * g + b


def _rms_norm(x, g):
    return x * lax.rsqrt(jnp.mean(x * x, axis=-1, keepdims=True) + RMS_EPS) * g


def _dot(a, b):
    return jnp.dot(a, b, preferred_element_type=F32)


def _ln0_kernel(x_ref, g_ref, b_ref, h_ref, hb_ref):
    h = _layer_norm(x_ref[...], g_ref[...], b_ref[...])
    h_ref[...] = h
    hb_ref[...] = h.astype(BF16)


def _ln0(x2, g, b):
    T = x2.shape[0]
    tm = TM_LN
    return pl.pallas_call(
        _ln0_kernel,
        grid=(T // tm,),
        in_specs=[pl.BlockSpec((tm, D_MODEL), lambda i: (i, 0)), _const_spec((1, D_MODEL)), _const_spec((1, D_MODEL))],
        out_specs=[pl.BlockSpec((tm, D_MODEL), lambda i: (i, 0)), pl.BlockSpec((tm, D_MODEL), lambda i: (i, 0))],
        out_shape=[jax.ShapeDtypeStruct((T, D_MODEL), F32), jax.ShapeDtypeStruct((T, D_MODEL), BF16)],
        compiler_params=_params(("parallel",)),
        name="ln0",
    )(x2, g, b)


def _mla_prep_kernel(hb_ref, wq_ref, bq_ref, gq_ref, wqu_ref, wqr_ref, wkv_ref, bkv_ref, gkv_ref, wku_ref, wvu_ref,
                     wkr_ref, bkr_ref, wkrr_ref, bkrr_ref, cq_ref, sq_ref, ck_ref, sk_ref,
                     qT_ref, k_ref, vT_ref):
    hb = hb_ref[...]
    cq = _rms_norm(_dot(hb, wq_ref[...]) + bq_ref[...], gq_ref[...]).astype(BF16)
    ckv = _rms_norm(_dot(hb, wkv_ref[...]) + bkv_ref[...], gkv_ref[...]).astype(BF16)
    cos_q, sin_q = cq_ref[...], sq_ref[...]
    k_rope = ((_dot(hb, wkr_ref[...]) + bkr_ref[...]) * ck_ref[...]
              + (_dot(hb, wkrr_ref[...]) + bkrr_ref[...]) * sk_ref[...])
    for h in range(N_HEADS):
        q_h = _dot(cq, wqu_ref[h]) * cos_q + _dot(cq, wqr_ref[h]) * sin_q
        qT_ref[0, h * HEAD_PAD:(h + 1) * HEAD_PAD, :] = q_h.T.astype(BF16)
        k_ref[0, h] = (_dot(ckv, wku_ref[h]) + k_rope).astype(BF16)
    v_all = _dot(ckv, wvu_ref[...])
    vT_ref[0] = v_all.T.astype(BF16)


def _mla_prep(hb, B, S, w):
    tm = TM_PREP
    nb = S // tm
    tok = lambda b, i: (b * nb + i, 0)
    pos = lambda b, i: (i, 0)
    in_specs = [
        pl.BlockSpec((tm, D_MODEL), tok),
        _const_spec((D_MODEL, Q_LORA)), _const_spec((1, Q_LORA)), _const_spec((1, Q_LORA)),
        _const_spec((N_HEADS, Q_LORA, HEAD_PAD)), _const_spec((N_HEADS, Q_LORA, HEAD_PAD)),
        _const_spec((D_MODEL, KV_LORA)), _const_spec((1, KV_LORA)), _const_spec((1, KV_LORA)),
        _const_spec((N_HEADS, KV_LORA, HEAD_PAD)), _const_spec((KV_LORA, N_HEADS * V_DIM)),
        _const_spec((D_MODEL, HEAD_PAD)), _const_spec((1, HEAD_PAD)),
        _const_spec((D_MODEL, HEAD_PAD)), _const_spec((1, HEAD_PAD)),
        pl.BlockSpec((tm, HEAD_PAD), pos), pl.BlockSpec((tm, HEAD_PAD), pos),
        pl.BlockSpec((tm, HEAD_PAD), pos), pl.BlockSpec((tm, HEAD_PAD), pos),
    ]
    out_specs = [
        pl.BlockSpec((1, N_HEADS * HEAD_PAD, tm), lambda b, i: (b, 0, i)),
        pl.BlockSpec((1, N_HEADS, tm, HEAD_PAD), lambda b, i: (b, 0, i, 0)),
        pl.BlockSpec((1, N_HEADS * V_DIM, tm), lambda b, i: (b, 0, i)),
    ]
    out_shape = [
        jax.ShapeDtypeStruct((B, N_HEADS * HEAD_PAD, S), BF16),
        jax.ShapeDtypeStruct((B, N_HEADS, S, HEAD_PAD), BF16),
        jax.ShapeDtypeStruct((B, N_HEADS * V_DIM, S), BF16),
    ]
    return pl.pallas_call(
        _mla_prep_kernel, grid=(B, nb), in_specs=in_specs, out_specs=out_specs, out_shape=out_shape,
        compiler_params=_params(("parallel", "parallel")), name="mla_prep",
    )(hb, w["wq"], w["bq"], w["gq"], w["wqu"], w["wqr"], w["wkv"], w["bkv"], w["gkv"], w["wku"], w["wvu"],
      w["wkr"], w["bkr"], w["wkrr"], w["bkrr"], w["cos_q"], w["sin_q"], w["cos_k"], w["sin_k"])


def _attn_kernel(qT_ref, k_ref, vT_ref, oT_ref, *, tk):
    qT = qT_ref[0]
    tq = qT.shape[1]
    n_chunks = k_ref.shape[2] // tk

    def body(c, carry):
        m, l, acc = carry
        off = pl.multiple_of(c * tk, tk)
        s = _dot(k_ref[0, 0, pl.ds(off, tk), :], qT)
        m_new = jnp.maximum(m, jnp.max(s, axis=0, keepdims=True))
        p = jnp.exp(s - m_new)
        alpha = jnp.exp(m - m_new)
        l = alpha * l + jnp.sum(p, axis=0, keepdims=True)
        acc = alpha * acc + _dot(vT_ref[0, :, pl.ds(off, tk)], p.astype(BF16))
        return m_new, l, acc

    init = (jnp.full((1, tq), -jnp.inf, F32), jnp.zeros((1, tq), F32), jnp.zeros((V_DIM, tq), F32))
    _, l, acc = lax.fori_loop(0, n_chunks, body, init)
    oT_ref[0] = acc * (1.0 / l)


def _attention(qT, k, vT):
    B, _, S = qT.shape
    tq = TQ_ATTN
    return pl.pallas_call(
        functools.partial(_attn_kernel, tk=min(TK_ATTN, S)),
        grid=(B, N_HEADS, S // tq),
        in_specs=[
            pl.BlockSpec((1, HEAD_PAD, tq), lambda b, h, i: (b, h, i)),
            pl.BlockSpec((1, 1, S, HEAD_PAD), lambda b, h, i: (b, h, 0, 0)),
            pl.BlockSpec((1, V_DIM, S), lambda b, h, i: (b, h, 0)),
        ],
        out_specs=pl.BlockSpec((1, V_DIM, tq), lambda b, h, i: (b, h, i)),
        out_shape=jax.ShapeDtypeStruct((B, N_HEADS * V_DIM, S), F32),
        compiler_params=_params(("parallel", "parallel", "parallel")),
        name="attention",
    )(qT, k, vT)


def _fgate_kernel(hb_ref, wz_ref, bz_ref, cs_ref, wg_ref, bg_ref, ur_ref, ui_ref, g_ref):
    hb = hb_ref[...]
    z = (_dot(hb, wz_ref[...]) + bz_ref[...]).astype(BF16)
    cs = cs_ref[...]
    for gi in range(F_GROUPS):
        lo, hi = gi * F_GROUP_DIM, (gi + 1) * F_GROUP_DIM
        pq = _dot(z[:, lo:hi], cs)
        ur_ref[:, lo:hi] = pq[:, :F_GROUP_DIM].astype(BF16)
        ui_ref[:, lo:hi] = (-pq[:, F_GROUP_DIM:]).astype(BF16)
    g_ref[...] = jax.nn.sigmoid(_dot(hb, wg_ref[...]) + bg_ref[...])


def _fgate(hb, w):
    T = hb.shape[0]
    tm = TM_FG
    row = lambda i: (i, 0)
    return pl.pallas_call(
        _fgate_kernel, grid=(T // tm,),
        in_specs=[pl.BlockSpec((tm, D_MODEL), row), _const_spec((D_MODEL, F_DIM)), _const_spec((1, F_DIM)),
                  _const_spec((F_GROUP_DIM, 2 * F_GROUP_DIM)), _const_spec((D_MODEL, 2 * D_MODEL)),
                  _const_spec((1, 2 * D_MODEL))],
        out_specs=[pl.BlockSpec((tm, F_DIM), row), pl.BlockSpec((tm, F_DIM), row), pl.BlockSpec((tm, 2 * D_MODEL), row)],
        out_shape=[jax.ShapeDtypeStruct((T, F_DIM), BF16), jax.ShapeDtypeStruct((T, F_DIM), BF16),
                   jax.ShapeDtypeStruct((T, 2 * D_MODEL), F32)],
        compiler_params=_params(("parallel",)), name="fgate",
    )(hb, w["wz"], w["bz"], w["cs128"], w["wg"], w["bg"])


def _fft_a_kernel(ur_ref, ui_ref, m1_ref, y_ref):
    u = jnp.concatenate([ur_ref[0], ui_ref[0]], axis=0)
    y_ref[0] = _dot(m1_ref[...], u).astype(BF16)


def _fft_a(ur3, ui3, m1):
    B, n1, W = ur3.shape
    L = min(FFT_A_LANES, W)
    blk = lambda b, j: (b, 0, j)
    return pl.pallas_call(
        _fft_a_kernel, grid=(B, W // L),
        in_specs=[pl.BlockSpec((1, n1, L), blk), pl.BlockSpec((1, n1, L), blk), _const_spec((2 * n1, 2 * n1))],
        out_specs=pl.BlockSpec((1, 2 * n1, L), blk),
        out_shape=jax.ShapeDtypeStruct((B, 2 * n1, W), BF16),
        compiler_params=_params(("parallel", "parallel")), name="fft_a",
    )(ur3, ui3, m1)


def _fft_b_kernel(y_ref, g_ref, o_ref, *, nk, scale):
    for j in range(nk):
        ycat = jnp.concatenate([y_ref[0, 0, j], y_ref[0, 1, j]], axis=0)
        o_ref[0, :, j * F_DIM:(j + 1) * F_DIM] = (_dot(g_ref[j], ycat) * scale).astype(BF16)


def _fft_b(y5, gtab, scale):
    B, _, n1, _, _ = y5.shape
    nk = min(FFT_B_K1, n1)
    return pl.pallas_call(
        functools.partial(_fft_b_kernel, nk=nk, scale=scale), grid=(B, n1 // nk),
        in_specs=[pl.BlockSpec((1, 2, nk, LANES, F_DIM), lambda b, j: (b, 0, j, 0, 0)),
                  pl.BlockSpec((nk, LANES, 2 * LANES), lambda b, j: (j, 0, 0))],
        out_specs=pl.BlockSpec((1, LANES, nk * F_DIM), lambda b, j: (b, 0, j)),
        out_shape=jax.ShapeDtypeStruct((B, LANES, n1 * F_DIM), BF16),
        compiler_params=_params(("parallel", "parallel")), name="fft_b",
    )(y5, gtab)


def _post_kernel(oT_ref, yf_ref, g_ref, h0_ref, wo_ref, wf_ref, wout_ref, bout_ref, lg_ref, lb_ref, h1_ref, h1T_ref):
    o = oT_ref[0].T.astype(BF16)
    y_a = _dot(o, wo_ref[...])
    y_f = _dot(yf_ref[...], wf_ref[...])
    g = g_ref[...]
    m = (g[:, :D_MODEL] * y_a + g[:, D_MODEL:] * y_f).astype(BF16)
    mix = _dot(m, wout_ref[...]) + bout_ref[...]
    h1 = _layer_norm(DN_ALPHA * h0_ref[...] + mix, lg_ref[...], lb_ref[...])
    h1_ref[...] = h1
    h1T_ref[...] = h1.T.astype(BF16)


def _post(oT, yf, g, h0, B, S, w):
    T = B * S
    tm = TM_POST
    nb = S // tm
    tok = lambda b, i: (b * nb + i, 0)
    return pl.pallas_call(
        _post_kernel, grid=(B, nb),
        in_specs=[pl.BlockSpec((1, N_HEADS * V_DIM, tm), lambda b, i: (b, 0, i)),
                  pl.BlockSpec((tm, F_DIM), tok), pl.BlockSpec((tm, 2 * D_MODEL), tok), pl.BlockSpec((tm, D_MODEL), tok),
                  _const_spec((N_HEADS * V_DIM, D_MODEL)), _const_spec((F_DIM, D_MODEL)), _const_spec((D_MODEL, D_MODEL)),
                  _const_spec((1, D_MODEL)), _const_spec((1, D_MODEL)), _const_spec((1, D_MODEL))],
        out_specs=[pl.BlockSpec((tm, D_MODEL), tok), pl.BlockSpec((D_MODEL, tm), lambda b, i: (0, b * nb + i))],
        out_shape=[jax.ShapeDtypeStruct((T, D_MODEL), F32), jax.ShapeDtypeStruct((D_MODEL, T), BF16)],
        compiler_params=_params(("parallel", "parallel")), name="post",
    )(oT, yf, g, h0, w["wo"], w["wf"], w["wout"], w["bout"], w["ln1_g"], w["ln1_b"])


def _extract_top16(s):
    row = lax.broadcasted_iota(jnp.int32, s.shape, 0).astype(F32)
    slot = lax.broadcasted_iota(jnp.int32, (PEER_TOPK, s.shape[1]), 0)
    rank = jnp.full(s.shape, float(PEER_TOPK), F32)
    vals = jnp.zeros((PEER_TOPK, s.shape[1]), F32)
    for r in range(PEER_TOPK):
        m = jnp.max(s, axis=0, keepdims=True)
        idx = jnp.min(jnp.where(s == m, row, float(PEER_NKEYS)), axis=0, keepdims=True)
        hit = row == idx
        s = jnp.where(hit, -jnp.inf, s)
        rank = jnp.where(hit, float(r), rank)
        vals = jnp.where(slot == r, m, vals)
    return vals, rank


def _select_pairs(a_vals, b_vals):
    L = a_vals.shape[1]
    sub = lax.broadcasted_iota(jnp.int32, (SUBLANES, L), 0)
    b_lo, b_hi = b_vals[:SUBLANES], b_vals[SUBLANES:]
    groups = []
    for (i, j0, valid) in _CAND_GROUPS:
        v = a_vals[i:i + 1, :] + (b_lo if j0 == 0 else b_hi)
        if valid < SUBLANES:
            v = jnp.where(sub < valid, v, -jnp.inf)
        groups.append(v)
    groups.append(a_vals[SUBLANES:] + b_vals[0:1, :])
    valid_rows = [g[2] for g in _CAND_GROUPS] + [SUBLANES]
    n_groups = len(groups)
    counts = [jnp.zeros((SUBLANES, L), F32) for _ in range(n_groups)]
    for gp in range(n_groups):
        for rp in range(valid_rows[gp]):
            vb = jnp.broadcast_to(groups[gp][rp:rp + 1, :], (SUBLANES, L))
            for g in range(n_groups):
                ge = jnp.where(vb >= groups[g], 1.0, 0.0)
                gt = jnp.where(vb > groups[g], 1.0, 0.0)
                if g < gp:
                    beats = gt
                elif g > gp:
                    beats = ge
                else:
                    beats = jnp.where(sub > rp, ge, gt)
                counts[g] = counts[g] + beats
    top = groups[0][0:1, :]
    sel = []
    z = jnp.zeros((1, L), F32)
    for g in range(n_groups):
        s_g = jnp.where(counts[g] < float(PEER_TOPK), 1.0, 0.0)
        if valid_rows[g] < SUBLANES:
            s_g = jnp.where(sub < valid_rows[g], s_g, 0.0)
        sel.append(s_g)
        z = z + jnp.sum(s_g * jnp.exp(groups[g] - top), axis=0, keepdims=True)
    n = [jnp.sum(sel[0] + sel[1], axis=0, keepdims=True)]
    for g in range(2, n_groups - 1):
        n.append(jnp.sum(sel[g], axis=0, keepdims=True))
    for r in range(SUBLANES):
        n.append(sel[n_groups - 1][r:r + 1, :])
    return n, z


def _peer_topk_kernel(h1T_ref, wqT_ref, keys_ref, n_ref, c_ref, r2_ref, e2_ref, qp_ref):
    tm = h1T_ref.shape[1]
    qp_ref[...] = _dot(wqT_ref[...], h1T_ref[...]).astype(BF16)

    def chunk(ci, _):
        lo = pl.multiple_of(ci * LANES, LANES)
        for h in range(PEER_HEADS):
            r0 = (2 * h) * PEER_HALF
            s1 = _dot(keys_ref[2 * h], qp_ref[pl.ds(r0, PEER_HALF), pl.ds(lo, LANES)])
            s2 = _dot(keys_ref[2 * h + 1], qp_ref[pl.ds(r0 + PEER_HALF, PEER_HALF), pl.ds(lo, LANES)])
            a_vals, rank1 = _extract_top16(s1)
            b_vals, rank2 = _extract_top16(s2)
            n, z = _select_pairs(a_vals, b_vals)
            n_a = jnp.zeros_like(s1)
            for i in range(PEER_TOPK):
                n_a = n_a + jnp.where(rank1 == float(i), n[i], 0.0)
            c_a = jnp.exp(s1 - a_vals[0:1, :]) * (1.0 / z)
            e2 = jnp.exp(s2 - b_vals[0:1, :])
            n_ref[h, :, pl.ds(lo, LANES)] = n_a
            c_ref[h, :, pl.ds(lo, LANES)] = c_a
            r2_ref[h, :, pl.ds(lo, LANES)] = rank2.astype(BF16)
            e2_ref[h, :, pl.ds(lo, LANES)] = e2.astype(BF16)
        return 0

    lax.fori_loop(0, tm // LANES, chunk, 0)


def _peer_topk(h1T, wqT, keys):
    T = h1T.shape[1]
    tm = TM_TOPK
    nq = wqT.shape[0]
    col = lambda i: (0, i)
    col3 = lambda i: (0, 0, i)
    return pl.pallas_call(
        _peer_topk_kernel, grid=(T // tm,),
        in_specs=[pl.BlockSpec((D_MODEL, tm), col), _const_spec((nq, D_MODEL)),
                  _const_spec((2 * PEER_HEADS, PEER_NKEYS, PEER_HALF))],
        out_specs=[pl.BlockSpec((PEER_HEADS, PEER_NKEYS, tm), col3), pl.BlockSpec((PEER_HEADS, PEER_NKEYS, tm), col3),
                   pl.BlockSpec((PEER_HEADS, PEER_NKEYS, tm), col3), pl.BlockSpec((PEER_HEADS, PEER_NKEYS, tm), col3)],
        out_shape=[jax.ShapeDtypeStruct((PEER_HEADS, PEER_NKEYS, T), F32),
                   jax.ShapeDtypeStruct((PEER_HEADS, PEER_NKEYS, T), F32),
                   jax.ShapeDtypeStruct((PEER_HEADS, PEER_NKEYS, T), BF16),
                   jax.ShapeDtypeStruct((PEER_HEADS, PEER_NKEYS, T), BF16)],
        scratch_shapes=[pltpu.VMEM((nq, tm), BF16)],
        compiler_params=_params(("parallel",)), name="peer_topk",
    )(h1T, wqT, keys)


def _peer_dense_kernel(xT_ref, wu_ref, wvT_ref, n_ref, c_ref, r2_ref, e2_ref, h1_ref, lg_ref, lb_ref, out_ref,
                       acc_ref, a_ref, *, eb):
    j = pl.program_id(1)
    tm = xT_ref.shape[1]

    @pl.when(j == 0)
    def _():
        acc_ref[...] = jnp.zeros_like(acc_ref)

    hpre = _dot(wu_ref[...], xT_ref[...])
    for aa in range(eb // PEER_NKEYS):
        a_key = j * (eb // PEER_NKEYS) + aa
        gate = jnp.zeros((PEER_NKEYS, tm), BF16)
        for h in range(PEER_HEADS):
            n_b = jnp.broadcast_to(n_ref[h, pl.ds(a_key, 1), :], (PEER_NKEYS, tm)).astype(BF16)
            c_b = jnp.broadcast_to(c_ref[h, pl.ds(a_key, 1), :], (PEER_NKEYS, tm)).astype(BF16)
            gate = gate + jnp.where(r2_ref[h] < n_b, e2_ref[h], jnp.zeros((), BF16)) * c_b
        hp = hpre[aa * PEER_NKEYS:(aa + 1) * PEER_NKEYS, :]
        act = 0.5 * hp * (1.0 + lax.erf(hp * (1.0 / math.sqrt(2.0))))
        a_ref[aa * PEER_NKEYS:(aa + 1) * PEER_NKEYS, :] = act.astype(BF16) * gate
    acc_ref[...] += _dot(wvT_ref[...], a_ref[...])

    @pl.when(j == pl.num_programs(1) - 1)
    def _():
        y = acc_ref[...].T
        out_ref[...] = _layer_norm(DN_ALPHA * h1_ref[...] + y, lg_ref[...], lb_ref[...])


def _peer_dense(h1T, wu, wvT, n_tab, c_tab, r2, e2, h1, lg, lb):
    T = h1T.shape[1]
    tm, eb = TM_PEER, EB_PEER
    col = lambda i, j: (0, i)
    col3 = lambda i, j: (0, 0, i)
    row = lambda i, j: (i, 0)
    return pl.pallas_call(
        functools.partial(_peer_dense_kernel, eb=eb), grid=(T // tm, PEER_EXPERTS // eb),
        in_specs=[pl.BlockSpec((D_MODEL, tm), col),
                  pl.BlockSpec((eb, D_MODEL), lambda i, j: (j, 0)),
                  pl.BlockSpec((D_MODEL, eb), lambda i, j: (0, j)),
                  pl.BlockSpec((PEER_HEADS, PEER_NKEYS, tm), col3), pl.BlockSpec((PEER_HEADS, PEER_NKEYS, tm), col3),
                  pl.BlockSpec((PEER_HEADS, PEER_NKEYS, tm), col3), pl.BlockSpec((PEER_HEADS, PEER_NKEYS, tm), col3),
                  pl.BlockSpec((tm, D_MODEL), row), _const_spec((1, D_MODEL)), _const_spec((1, D_MODEL))],
        out_specs=pl.BlockSpec((tm, D_MODEL), row),
        out_shape=jax.ShapeDtypeStruct((T, D_MODEL), F32),
        scratch_shapes=[pltpu.VMEM((D_MODEL, tm), F32), pltpu.VMEM((eb, tm), BF16)],
        compiler_params=_params(("parallel", "arbitrary")), name="peer_dense",
    )(h1T, wu, wvT, n_tab, c_tab, r2, e2, h1, lg, lb)


def _pad_cols(w, lo, total):
    return jnp.pad(w, ((0, 0), (lo, total - lo - w.shape[1])))


def _prepare(S, w_in, b_in, q_norm_g, kv_norm_g, w_uq, w_uk, w_uv, w_o_attn, w_fourier, w_out, b_out, ln1_g, ln1_b,
             peer_w_q, peer_keys, peer_w_u, peer_w_v):
    o_kv, o_kr, o_z, o_g = Q_LORA, Q_LORA + KV_LORA, Q_LORA + KV_LORA + QK_ROPE_DIM, Q_LORA + KV_LORA + QK_ROPE_DIM + F_DIM
    row = lambda v: v.reshape(1, -1).astype(F32)
    w = {}
    w["wq"], w["bq"], w["gq"] = w_in[:, :o_kv].astype(BF16), row(b_in[:o_kv]), row(q_norm_g)
    w["wkv"], w["bkv"], w["gkv"] = w_in[:, o_kv:o_kr].astype(BF16), row(b_in[o_kv:o_kr]), row(kv_norm_g)
    wkr, bkr = w_in[:, o_kr:o_z], b_in[o_kr:o_z].reshape(1, -1)
    swap = lambda m: jnp.concatenate([m[:, ROPE_HALF:], m[:, :ROPE_HALF]], axis=1)
    w["wkr"], w["bkr"] = _pad_cols(wkr, QK_NOPE_DIM, HEAD_PAD).astype(BF16), _pad_cols(bkr, QK_NOPE_DIM, HEAD_PAD).astype(F32)
    w["wkrr"] = _pad_cols(swap(wkr), QK_NOPE_DIM, HEAD_PAD).astype(BF16)
    w["bkrr"] = _pad_cols(swap(bkr), QK_NOPE_DIM, HEAD_PAD).astype(F32)
    w["wz"], w["bz"] = w_in[:, o_z:o_g].astype(BF16), row(b_in[o_z:o_g])
    w["wg"], w["bg"] = w_in[:, o_g:].astype(BF16), row(b_in[o_g:])
    wq3 = w_uq.reshape(Q_LORA, N_HEADS, QK_DIM).transpose(1, 0, 2)
    rope3 = wq3[:, :, QK_NOPE_DIM:]
    rope3_sw = jnp.concatenate([rope3[:, :, ROPE_HALF:], rope3[:, :, :ROPE_HALF]], axis=2)
    padh = lambda m: jnp.pad(m, ((0, 0), (0, 0), (0, HEAD_PAD - m.shape[2])))
    w["wqu"] = padh(wq3).astype(BF16)
    w["wqr"] = padh(jnp.concatenate([jnp.zeros_like(wq3[:, :, :QK_NOPE_DIM]), rope3_sw], axis=2)).astype(BF16)
    wk3 = w_uk.reshape(KV_LORA, N_HEADS, QK_NOPE_DIM).transpose(1, 0, 2)
    w["wku"] = padh(wk3).astype(BF16)
    w["wvu"] = w_uv.astype(BF16)
    pos = jnp.arange(S, dtype=F32)
    inv = 1.0 / (ROPE_THETA ** (jnp.arange(0, QK_ROPE_DIM, 2, dtype=F32) / QK_ROPE_DIM))
    ang = pos[:, None] * inv[None, :]
    cos, sin = jnp.cos(ang), jnp.sin(ang)
    ones, zeros = jnp.ones((S, QK_NOPE_DIM), F32), jnp.zeros((S, QK_NOPE_DIM), F32)
    tail = jnp.zeros((S, HEAD_PAD - QK_DIM), F32)
    scale = QK_DIM ** -0.5
    w["cos_q"] = jnp.concatenate([ones, cos, cos, tail], axis=1) * scale
    w["sin_q"] = jnp.concatenate([zeros, -sin, sin, tail], axis=1) * scale
    w["cos_k"] = jnp.concatenate([zeros, cos, cos, tail], axis=1)
    w["sin_k"] = jnp.concatenate([zeros, -sin, sin, tail], axis=1)
    n1 = S // LANES
    idx = lambda n: jnp.arange(n, dtype=jnp.int32)
    ang_c = (2.0 * math.pi / F_GROUP_DIM) * ((idx(F_GROUP_DIM)[:, None] * idx(F_GROUP_DIM)[None, :]) % F_GROUP_DIM).astype(F32)
    w["cs128"] = jnp.concatenate([jnp.cos(ang_c), jnp.sin(ang_c)], axis=1).astype(BF16)
    ang_1 = (2.0 * math.pi / n1) * ((idx(n1)[:, None] * idx(n1)[None, :]) % n1).astype(F32)
    c1, s1 = jnp.cos(ang_1), jnp.sin(ang_1)
    w["m1"] = jnp.concatenate([jnp.concatenate([c1, s1], axis=1), jnp.concatenate([-s1, c1], axis=1)], axis=0).astype(BF16)
    kk = idx(n1)[:, None, None] + n1 * idx(LANES)[None, :, None]
    ang_g = (2.0 * math.pi / S) * ((kk * idx(LANES)[None, None, :]) % S).astype(F32)
    w["gtab"] = jnp.concatenate([jnp.cos(ang_g), jnp.sin(ang_g)], axis=2).astype(BF16)
    w["wo"], w["wf"], w["wout"] = w_o_attn.astype(BF16), w_fourier.astype(BF16), w_out.astype(BF16)
    w["bout"], w["ln1_g"], w["ln1_b"] = row(b_out), row(ln1_g), row(ln1_b)
    w["wqT"] = peer_w_q.T.astype(BF16)
    w["keys"] = peer_keys.reshape(2 * PEER_HEADS, PEER_NKEYS, PEER_HALF).astype(BF16)
    w["wu"] = peer_w_u.astype(BF16)
    w["wvT"] = peer_w_v.T.astype(BF16)
    return w


def kernel(x, ln0_g, ln0_b, w_in, b_in, q_norm_g, kv_norm_g, w_uq, w_uk, w_uv, w_o_attn, w_fourier, w_out, b_out, ln1_g,
           ln1_b, peer_w_q, peer_keys, peer_w_u, peer_w_v, ln2_g, ln2_b):
    B, S, D = x.shape
    assert D == D_MODEL and w_in.shape[0] == DEPTH
    T = B * S
    n1 = S // LANES
    assert S % max(TM_PREP, TQ_ATTN, TM_POST, LANES * SUBLANES) == 0 and T % max(TM_LN, TM_PEER, TM_FG, TM_TOPK) == 0
    row = lambda v: v.reshape(1, -1).astype(F32)
    h = x.reshape(T, D)
    w = _prepare(S, w_in[0], b_in[0], q_norm_g[0], kv_norm_g[0], w_uq[0], w_uk[0], w_uv[0], w_o_attn[0], w_fourier[0],
                 w_out[0], b_out[0], ln1_g[0], ln1_b[0], peer_w_q[0], peer_keys[0], peer_w_u[0], peer_w_v[0])
    h0, h0b = _ln0(h, row(ln0_g), row(ln0_b))
    qT, k, vT = _mla_prep(h0b, B, S, w)
    oT = _attention(qT, k, vT)
    ur, ui, g = _fgate(h0b, w)
    y = _fft_a(ur.reshape(B, n1, LANES * F_DIM), ui.reshape(B, n1, LANES * F_DIM), w["m1"])
    yf = _fft_b(y.reshape(B, 2, n1, LANES, F_DIM), w["gtab"], 1.0 / math.sqrt(S * F_GROUP_DIM))
    h1, h1T = _post(oT, yf.reshape(T, F_DIM), g, h0, B, S, w)
    n_tab, c_tab, r2, e2 = _peer_topk(h1T, w["wqT"], w["keys"])
    out = _peer_dense(h1T, w["wu"], w["wvT"], n_tab, c_tab, r2, e2, h1, row(ln2_g[0]), row(ln2_b[0]))
    return out.reshape(B, S, D)
```

```python
import functools
import math

import jax
import jax.numpy as jnp
from jax import lax
from jax.experimental import pallas as pl
from jax.experimental.pallas import tpu as pltpu

F32 = jnp.float32
BF16 = jnp.bfloat16

D_MODEL = 1024
N_HEADS = 8
QK_NOPE_DIM = 64
QK_ROPE_DIM = 32
ROPE_HALF = QK_ROPE_DIM // 2
QK_DIM = QK_NOPE_DIM + QK_ROPE_DIM
V_DIM = 64
Q_LORA = 256
KV_LORA = 256
ROPE_THETA = 10000.0
F_GROUPS = 4
F_GROUP_DIM = 128
F_DIM = F_GROUPS * F_GROUP_DIM
PEER_HEADS = 8
PEER_NKEYS = 128
PEER_EXPERTS = PEER_NKEYS * PEER_NKEYS
PEER_HALF = 128
PEER_TOPK = 16
DEPTH = 1
DN_ALPHA = (2.0 * DEPTH) ** 0.25
LN_EPS = 1e-5
RMS_EPS = 1e-6

LANES = 128
SUBLANES = 8
HEAD_PAD = LANES
VMEM_LIMIT_BYTES = 56 * 1024 * 1024
INTERLEAVE_FLAGS = None

TM_LN = 512
TM_PREP = 256
TQ_ATTN = 1024
TQ_SUB = 256
TK_ATTN = 512
TM_FG = 256
FFT_A_LANES = 4096
FFT_B_K1 = 8
TM_POST = 256
TM_TOPK = 256
TM_PEER = 512
EB_PEER = 1024
PEER_CHAIN = 256

_CAND_GROUPS = ((0, 0, 8), (0, 8, 8), (1, 0, 8), (2, 0, 5), (3, 0, 4), (4, 0, 3), (5, 0, 2), (6, 0, 2), (7, 0, 2))


def _params(sem, flags=None):
    return pltpu.CompilerParams(dimension_semantics=sem, vmem_limit_bytes=VMEM_LIMIT_BYTES, flags=flags)


def _const_spec(shape):
    nd = len(shape)
    return pl.BlockSpec(shape, lambda *_: (0,) * nd)


def _layer_norm(x, g, b):
    mu = jnp.mean(x, axis=-1, keepdims=True)
    xc = x - mu
    var = jnp.mean(xc * xc, axis=-1, keepdims=True)
    return xc * lax.rsqrt(var + LN_EPS) * g + b


def _rms_norm(x, g):
    return x * lax.rsqrt(jnp.mean(x * x, axis=-1, keepdims=True) + RMS_EPS) * g


def _dot(a, b):
    return jnp.dot(a, b, preferred_element_type=F32)


def _ln0_kernel(x_ref, g_ref, b_ref, h_ref, hb_ref):
    h = _layer_norm(x_ref[...], g_ref[...], b_ref[...])
    h_ref[...] = h
    hb_ref[...] = h.astype(BF16)


def _ln0(x2, g, b):
    T = x2.shape[0]
    tm = TM_LN
    return pl.pallas_call(
        _ln0_kernel,
        grid=(T // tm,),
        in_specs=[pl.BlockSpec((tm, D_MODEL), lambda i: (i, 0)), _const_spec((1, D_MODEL)), _const_spec((1, D_MODEL))],
        out_specs=[pl.BlockSpec((tm, D_MODEL), lambda i: (i, 0)), pl.BlockSpec((tm, D_MODEL), lambda i: (i, 0))],
        out_shape=[jax.ShapeDtypeStruct((T, D_MODEL), F32), jax.ShapeDtypeStruct((T, D_MODEL), BF16)],
        compiler_params=_params(("parallel",)),
        name="ln0",
    )(x2, g, b)


def _mla_prep_kernel(hb_ref, wq_ref, bq_ref, gq_ref, wqu_ref, wqr_ref, wkv_ref, bkv_ref, gkv_ref, wku_ref, wvu_ref,
                     wkr_ref, bkr_ref, wkrr_ref, bkrr_ref, cq_ref, sq_ref, ck_ref, sk_ref,
                     qT_ref, k_ref, vT_ref):
    hb = hb_ref[...]
    cq = _rms_norm(_dot(hb, wq_ref[...]) + bq_ref[...], gq_ref[...]).astype(BF16)
    ckv = _rms_norm(_dot(hb, wkv_ref[...]) + bkv_ref[...], gkv_ref[...]).astype(BF16)
    cos_q, sin_q = cq_ref[...], sq_ref[...]
    k_rope = ((_dot(hb, wkr_ref[...]) + bkr_ref[...]) * ck_ref[...]
              + (_dot(hb, wkrr_ref[...]) + bkrr_ref[...]) * sk_ref[...])
    for h in range(N_HEADS):
        q_h = _dot(cq, wqu_ref[h]) * cos_q + _dot(cq, wqr_ref[h]) * sin_q
        qT_ref[0, h * HEAD_PAD:(h + 1) * HEAD_PAD, :] = q_h.T.astype(BF16)
        k_ref[0, h] = (_dot(ckv, wku_ref[h]) + k_rope).astype(BF16)
    v_all = _dot(ckv, wvu_ref[...])
    vT_ref[0] = v_all.T.astype(BF16)


def _mla_prep(hb, B, S, w):
    tm = TM_PREP
    nb = S // tm
    tok = lambda b, i: (b * nb + i, 0)
    pos = lambda b, i: (i, 0)
    in_specs = [
        pl.BlockSpec((tm, D_MODEL), tok),
        _const_spec((D_MODEL, Q_LORA)), _const_spec((1, Q_LORA)), _const_spec((1, Q_LORA)),
        _const_spec((N_HEADS, Q_LORA, HEAD_PAD)), _const_spec((N_HEADS, Q_LORA, HEAD_PAD)),
        _const_spec((D_MODEL, KV_LORA)), _const_spec((1, KV_LORA)), _const_spec((1, KV_LORA)),
        _const_spec((N_HEADS, KV_LORA, HEAD_PAD)), _const_spec((KV_LORA, N_HEADS * V_DIM)),
        _const_spec((D_MODEL, HEAD_PAD)), _const_spec((1, HEAD_PAD)),
        _const_spec((D_MODEL, HEAD_PAD)), _const_spec((1, HEAD_PAD)),
        pl.BlockSpec((tm, HEAD_PAD), pos), pl.BlockSpec((tm, HEAD_PAD), pos),
        pl.BlockSpec((tm, HEAD_PAD), pos), pl.BlockSpec((tm, HEAD_PAD), pos),
    ]
    out_specs = [
        pl.BlockSpec((1, N_HEADS * HEAD_PAD, tm), lambda b, i: (b, 0, i)),
        pl.BlockSpec((1, N_HEADS, tm, HEAD_PAD), lambda b, i: (b, 0, i, 0)),
        pl.BlockSpec((1, N_HEADS * V_DIM, tm), lambda b, i: (b, 0, i)),
    ]
    out_shape = [
        jax.ShapeDtypeStruct((B, N_HEADS * HEAD_PAD, S), BF16),
        jax.ShapeDtypeStruct((B, N_HEADS, S, HEAD_PAD), BF16),
        jax.ShapeDtypeStruct((B, N_HEADS * V_DIM, S), BF16),
    ]
    return pl.pallas_call(
        _mla_prep_kernel, grid=(B, nb), in_specs=in_specs, out_specs=out_specs, out_shape=out_shape,
        compiler_params=_params(("parallel", "parallel")), name="mla_prep",
    )(hb, w["wq"], w["bq"], w["gq"], w["wqu"], w["wqr"], w["wkv"], w["bkv"], w["gkv"], w["wku"], w["wvu"],
      w["wkr"], w["bkr"], w["wkrr"], w["bkrr"], w["cos_q"], w["sin_q"], w["cos_k"], w["sin_k"])


def _col_reduce(x, op, final):
    parts = [x[i:i + SUBLANES] for i in range(0, x.shape[0], SUBLANES)]
    while len(parts) > 1:
        parts = [op(parts[i], parts[i + 1]) for i in range(0, len(parts), 2)]
    return final(parts[0], axis=0, keepdims=True)


def _attn_kernel(qT_ref, k_ref, vT_ref, oT_ref, s_ref, *, tk, n_sub):
    tq = qT_ref.shape[2]
    tsub = tq // n_sub
    n_chunks = k_ref.shape[2] // tk
    subs = range(n_sub)

    def scores(c, slot):
        off = pl.multiple_of(c * tk, tk)
        k_c = k_ref[0, 0, pl.ds(off, tk), :]
        for u in subs:
            s_ref[slot, u] = _dot(k_c, qT_ref[0, :, u * tsub:(u + 1) * tsub])

    def softmax_pv(c, slot, carry):
        off = pl.multiple_of(c * tk, tk)
        vT_c = vT_ref[0, :, pl.ds(off, tk)]
        m_new = [jnp.maximum(carry[u][0], _col_reduce(s_ref[slot, u], jnp.maximum, jnp.max)) for u in subs]
        p = [jnp.exp2(s_ref[slot, u] - m_new[u]) for u in subs]
        alpha = [jnp.exp2(carry[u][0] - m_new[u]) for u in subs]
        l = [alpha[u] * carry[u][1] + _col_reduce(p[u], jnp.add, jnp.sum) for u in subs]
        acc = [alpha[u] * carry[u][2] + _dot(vT_c, p[u].astype(BF16)) for u in subs]
        return tuple((m_new[u], l[u], acc[u]) for u in subs)

    def body(i, carry):
        c0 = 2 * i
        scores(c0 + 1, 1)
        carry = softmax_pv(c0, 0, carry)
        scores(jnp.minimum(c0 + 2, n_chunks - 1), 0)
        return softmax_pv(c0 + 1, 1, carry)

    scores(0, 0)
    init = tuple((jnp.full((1, tsub), -jnp.inf, F32), jnp.zeros((1, tsub), F32), jnp.zeros((V_DIM, tsub), F32))
                 for _ in range(n_sub))
    fin = lax.fori_loop(0, n_chunks // 2, body, init)
    for u in range(n_sub):
        _, l, acc = fin[u]
        oT_ref[0, :, u * tsub:(u + 1) * tsub] = acc * (1.0 / l)


def _attention(qT, k, vT):
    B, _, S = qT.shape
    tq = TQ_ATTN
    return pl.pallas_call(
        functools.partial(_attn_kernel, tk=min(TK_ATTN, S), n_sub=TQ_ATTN // TQ_SUB),
        grid=(B, N_HEADS, S // tq),
        in_specs=[
            pl.BlockSpec((1, HEAD_PAD, tq), lambda b, h, i: (b, h, i)),
            pl.BlockSpec((1, 1, S, HEAD_PAD), lambda b, h, i: (b, h, 0, 0)),
            pl.BlockSpec((1, V_DIM, S), lambda b, h, i: (b, h, 0)),
        ],
        out_specs=pl.BlockSpec((1, V_DIM, tq), lambda b, h, i: (b, h, i)),
        out_shape=jax.ShapeDtypeStruct((B, N_HEADS * V_DIM, S), F32),
        scratch_shapes=[pltpu.VMEM((2, TQ_ATTN // TQ_SUB, min(TK_ATTN, S), TQ_SUB), F32)],
        compiler_params=_params(("parallel", "parallel", "parallel"), INTERLEAVE_FLAGS),
        name="attention",
    )(qT, k, vT)


def _fgate_kernel(hb_ref, wz_ref, bz_ref, cs_ref, wg_ref, bg_ref, ur_ref, ui_ref, g_ref):
    hb = hb_ref[...]
    z = (_dot(hb, wz_ref[...]) + bz_ref[...]).astype(BF16)
    cs = cs_ref[...]
    for gi in range(F_GROUPS):
        lo, hi = gi * F_GROUP_DIM, (gi + 1) * F_GROUP_DIM
        pq = _dot(z[:, lo:hi], cs)
        ur_ref[:, lo:hi] = pq[:, :F_GROUP_DIM].astype(BF16)
        ui_ref[:, lo:hi] = (-pq[:, F_GROUP_DIM:]).astype(BF16)
    g_ref[...] = jax.nn.sigmoid(_dot(hb, wg_ref[...]) + bg_ref[...])


def _fgate(hb, w):
    T = hb.shape[0]
    tm = TM_FG
    row = lambda i: (i, 0)
    return pl.pallas_call(
        _fgate_kernel, grid=(T // tm,),
        in_specs=[pl.BlockSpec((tm, D_MODEL), row), _const_spec((D_MODEL, F_DIM)), _const_spec((1, F_DIM)),
                  _const_spec((F_GROUP_DIM, 2 * F_GROUP_DIM)), _const_spec((D_MODEL, 2 * D_MODEL)),
                  _const_spec((1, 2 * D_MODEL))],
        out_specs=[pl.BlockSpec((tm, F_DIM), row), pl.BlockSpec((tm, F_DIM), row), pl.BlockSpec((tm, 2 * D_MODEL), row)],
        out_shape=[jax.ShapeDtypeStruct((T, F_DIM), BF16), jax.ShapeDtypeStruct((T, F_DIM), BF16),
                   jax.ShapeDtypeStruct((T, 2 * D_MODEL), F32)],
        compiler_params=_params(("parallel",)), name="fgate",
    )(hb, w["wz"], w["bz"], w["cs128"], w["wg"], w["bg"])


def _fft_a_kernel(ur_ref, ui_ref, m1_ref, y_ref):
    u = jnp.concatenate([ur_ref[0], ui_ref[0]], axis=0)
    y_ref[0] = _dot(m1_ref[...], u).astype(BF16)


def _fft_a(ur3, ui3, m1):
    B, n1, W = ur3.shape
    L = min(FFT_A_LANES, W)
    blk = lambda b, j: (b, 0, j)
    return pl.pallas_call(
        _fft_a_kernel, grid=(B, W // L),
        in_specs=[pl.BlockSpec((1, n1, L), blk), pl.BlockSpec((1, n1, L), blk), _const_spec((2 * n1, 2 * n1))],
        out_specs=pl.BlockSpec((1, 2 * n1, L), blk),
        out_shape=jax.ShapeDtypeStruct((B, 2 * n1, W), BF16),
        compiler_params=_params(("parallel", "parallel")), name="fft_a",
    )(ur3, ui3, m1)


def _fft_b_kernel(y_ref, g_ref, o_ref, *, nk, scale):
    for j in range(nk):
        ycat = jnp.concatenate([y_ref[0, 0, j], y_ref[0, 1, j]], axis=0)
        o_ref[0, :, j * F_DIM:(j + 1) * F_DIM] = (_dot(g_ref[j], ycat) * scale).astype(BF16)


def _fft_b(y5, gtab, scale):
    B, _, n1, _, _ = y5.shape
    nk = min(FFT_B_K1, n1)
    return pl.pallas_call(
        functools.partial(_fft_b_kernel, nk=nk, scale=scale), grid=(B, n1 // nk),
        in_specs=[pl.BlockSpec((1, 2, nk, LANES, F_DIM), lambda b, j: (b, 0, j, 0, 0)),
                  pl.BlockSpec((nk, LANES, 2 * LANES), lambda b, j: (j, 0, 0))],
        out_specs=pl.BlockSpec((1, LANES, nk * F_DIM), lambda b, j: (b, 0, j)),
        out_shape=jax.ShapeDtypeStruct((B, LANES, n1 * F_DIM), BF16),
        compiler_params=_params(("parallel", "parallel")), name="fft_b",
    )(y5, gtab)


def _post_kernel(oT_ref, yf_ref, g_ref, h0_ref, wo_ref, wf_ref, wout_ref, bout_ref, lg_ref, lb_ref, h1_ref, h1T_ref):
    o = oT_ref[0].T.astype(BF16)
    y_a = _dot(o, wo_ref[...])
    y_f = _dot(yf_ref[...], wf_ref[...])
    g = g_ref[...]
    m = (g[:, :D_MODEL] * y_a + g[:, D_MODEL:] * y_f).astype(BF16)
    mix = _dot(m, wout_ref[...]) + bout_ref[...]
    h1 = _layer_norm(DN_ALPHA * h0_ref[...] + mix, lg_ref[...], lb_ref[...])
    h1_ref[...] = h1
    h1T_ref[...] = h1.T.astype(BF16)


def _post(oT, yf, g, h0, B, S, w):
    T = B * S
    tm = TM_POST
    nb = S // tm
    tok = lambda b, i: (b * nb + i, 0)
    return pl.pallas_call(
        _post_kernel, grid=(B, nb),
        in_specs=[pl.BlockSpec((1, N_HEADS * V_DIM, tm), lambda b, i: (b, 0, i)),
                  pl.BlockSpec((tm, F_DIM), tok), pl.BlockSpec((tm, 2 * D_MODEL), tok), pl.BlockSpec((tm, D_MODEL), tok),
                  _const_spec((N_HEADS * V_DIM, D_MODEL)), _const_spec((F_DIM, D_MODEL)), _const_spec((D_MODEL, D_MODEL)),
                  _const_spec((1, D_MODEL)), _const_spec((1, D_MODEL)), _const_spec((1, D_MODEL))],
        out_specs=[pl.BlockSpec((tm, D_MODEL), tok), pl.BlockSpec((D_MODEL, tm), lambda b, i: (0, b * nb + i))],
        out_shape=[jax.ShapeDtypeStruct((T, D_MODEL), F32), jax.ShapeDtypeStruct((D_MODEL, T), BF16)],
        compiler_params=_params(("parallel", "parallel")), name="post",
    )(oT, yf, g, h0, w["wo"], w["wf"], w["wout"], w["bout"], w["ln1_g"], w["ln1_b"])


def _extract_top16(s):
    row = lax.broadcasted_iota(jnp.int32, s.shape, 0).astype(F32)
    slot = lax.broadcasted_iota(jnp.int32, (PEER_TOPK, s.shape[1]), 0)
    rank = jnp.full(s.shape, float(PEER_TOPK), F32)
    vals = jnp.zeros((PEER_TOPK, s.shape[1]), F32)
    for r in range(PEER_TOPK):
        m = jnp.max(s, axis=0, keepdims=True)
        idx = jnp.min(jnp.where(s == m, row, float(PEER_NKEYS)), axis=0, keepdims=True)
        hit = row == idx
        s = jnp.where(hit, -jnp.inf, s)
        rank = jnp.where(hit, float(r), rank)
        vals = jnp.where(slot == r, m, vals)
    return vals, rank


def _select_pairs(a_vals, b_vals):
    L = a_vals.shape[1]
    sub = lax.broadcasted_iota(jnp.int32, (SUBLANES, L), 0)
    b_lo, b_hi = b_vals[:SUBLANES], b_vals[SUBLANES:]
    groups = []
    for (i, j0, valid) in _CAND_GROUPS:
        v = a_vals[i:i + 1, :] + (b_lo if j0 == 0 else b_hi)
        if valid < SUBLANES:
            v = jnp.where(sub < valid, v, -jnp.inf)
        groups.append(v)
    groups.append(a_vals[SUBLANES:] + b_vals[0:1, :])
    valid_rows = [g[2] for g in _CAND_GROUPS] + [SUBLANES]
    n_groups = len(groups)
    counts = [jnp.zeros((SUBLANES, L), F32) for _ in range(n_groups)]
    for gp in range(n_groups):
        for rp in range(valid_rows[gp]):
            vb = jnp.broadcast_to(groups[gp][rp:rp + 1, :], (SUBLANES, L))
            for g in range(n_groups):
                ge = jnp.where(vb >= groups[g], 1.0, 0.0)
                gt = jnp.where(vb > groups[g], 1.0, 0.0)
                if g < gp:
                    beats = gt
                elif g > gp:
                    beats = ge
                else:
                    beats = jnp.where(sub > rp, ge, gt)
                counts[g] = counts[g] + beats
    top = groups[0][0:1, :]
    sel = []
    z = jnp.zeros((1, L), F32)
    for g in range(n_groups):
        s_g = jnp.where(counts[g] < float(PEER_TOPK), 1.0, 0.0)
        if valid_rows[g] < SUBLANES:
            s_g = jnp.where(sub < valid_rows[g], s_g, 0.0)
        sel.append(s_g)
        z = z + jnp.sum(s_g * jnp.exp(groups[g] - top), axis=0, keepdims=True)
    n = [jnp.sum(sel[0] + sel[1], axis=0, keepdims=True)]
    for g in range(2, n_groups - 1):
        n.append(jnp.sum(sel[g], axis=0, keepdims=True))
    for r in range(SUBLANES):
        n.append(sel[n_groups - 1][r:r + 1, :])
    return n, z


def _peer_topk_kernel(h1T_ref, wqT_ref, keys_ref, n_ref, c_ref, r2_ref, e2_ref, qp_ref):
    tm = h1T_ref.shape[1]
    qp_ref[...] = _dot(wqT_ref[...], h1T_ref[...]).astype(BF16)

    def chunk(ci, _):
        lo = pl.multiple_of(ci * LANES, LANES)
        for h in range(PEER_HEADS):
            r0 = (2 * h) * PEER_HALF
            s1 = _dot(keys_ref[2 * h], qp_ref[pl.ds(r0, PEER_HALF), pl.ds(lo, LANES)])
            s2 = _dot(keys_ref[2 * h + 1], qp_ref[pl.ds(r0 + PEER_HALF, PEER_HALF), pl.ds(lo, LANES)])
            a_vals, rank1 = _extract_top16(s1)
            b_vals, rank2 = _extract_top16(s2)
            n, z = _select_pairs(a_vals, b_vals)
            n_a = jnp.zeros_like(s1)
            for i in range(PEER_TOPK):
                n_a = n_a + jnp.where(rank1 == float(i), n[i], 0.0)
            c_a = jnp.exp(s1 - a_vals[0:1, :]) * (1.0 / z)
            e2 = jnp.exp(s2 - b_vals[0:1, :])
            n_ref[h, :, pl.ds(lo, LANES)] = n_a
            c_ref[h, :, pl.ds(lo, LANES)] = c_a
            r2_ref[h, :, pl.ds(lo, LANES)] = rank2.astype(BF16)
            e2_ref[h, :, pl.ds(lo, LANES)] = e2.astype(BF16)
        return 0

    lax.fori_loop(0, tm // LANES, chunk, 0)


def _peer_topk(h1T, wqT, keys):
    T = h1T.shape[1]
    tm = TM_TOPK
    nq = wqT.shape[0]
    col = lambda i: (0, i)
    col3 = lambda i: (0, 0, i)
    return pl.pallas_call(
        _peer_topk_kernel, grid=(T // tm,),
        in_specs=[pl.BlockSpec((D_MODEL, tm), col), _const_spec((nq, D_MODEL)),
                  _const_spec((2 * PEER_HEADS, PEER_NKEYS, PEER_HALF))],
        out_specs=[pl.BlockSpec((PEER_HEADS, PEER_NKEYS, tm), col3), pl.BlockSpec((PEER_HEADS, PEER_NKEYS, tm), col3),
                   pl.BlockSpec((PEER_HEADS, PEER_NKEYS, tm), col3), pl.BlockSpec((PEER_HEADS, PEER_NKEYS, tm), col3)],
        out_shape=[jax.ShapeDtypeStruct((PEER_HEADS, PEER_NKEYS, T), F32),
                   jax.ShapeDtypeStruct((PEER_HEADS, PEER_NKEYS, T), F32),
                   jax.ShapeDtypeStruct((PEER_HEADS, PEER_NKEYS, T), BF16),
                   jax.ShapeDtypeStruct((PEER_HEADS, PEER_NKEYS, T), BF16)],
        scratch_shapes=[pltpu.VMEM((nq, tm), BF16)],
        compiler_params=_params(("parallel",)), name="peer_topk",
    )(h1T, wqT, keys)


def _peer_dense_kernel(xT_ref, wu_ref, wvT_ref, n_ref, c_ref, r2_ref, e2_ref, h1_ref, lg_ref, lb_ref, out_ref,
                       acc_ref, *, eb):
    j = pl.program_id(1)
    tm = xT_ref.shape[1]

    @pl.when(j == 0)
    def _():
        acc_ref[...] = jnp.zeros_like(acc_ref)

    n_chains = eb // PEER_CHAIN
    rows = [slice(ch * PEER_CHAIN, (ch + 1) * PEER_CHAIN) for ch in range(n_chains)]

    def first_matmul(ch):
        return _dot(wu_ref[rows[ch], :], xT_ref[...])

    def gate_gelu(ch, hpre):
        acts = []
        for aa in range(PEER_CHAIN // PEER_NKEYS):
            a_key = (j * n_chains + ch) * (PEER_CHAIN // PEER_NKEYS) + aa
            gate = jnp.zeros((PEER_NKEYS, tm), BF16)
            for h in range(PEER_HEADS):
                n_b = jnp.broadcast_to(n_ref[h, pl.ds(a_key, 1), :], (PEER_NKEYS, tm)).astype(BF16)
                c_b = jnp.broadcast_to(c_ref[h, pl.ds(a_key, 1), :], (PEER_NKEYS, tm)).astype(BF16)
                gate = gate + jnp.where(r2_ref[h] < n_b, e2_ref[h], jnp.zeros((), BF16)) * c_b
            hp = hpre[aa * PEER_NKEYS:(aa + 1) * PEER_NKEYS, :]
            act = 0.5 * hp * (1.0 + lax.erf(hp * (1.0 / math.sqrt(2.0))))
            acts.append(act.astype(BF16) * gate)
        return jnp.concatenate(acts, axis=0)

    hpre, act = {}, {}
    for t in range(n_chains + 2):
        if t < n_chains:
            hpre[t] = first_matmul(t)
        if 0 <= t - 1 < n_chains:
            act[t - 1] = gate_gelu(t - 1, hpre.pop(t - 1))
        if 0 <= t - 2 < n_chains:
            acc_ref[...] += _dot(wvT_ref[:, rows[t - 2]], act.pop(t - 2))

    @pl.when(j == pl.num_programs(1) - 1)
    def _():
        y = acc_ref[...].T
        out_ref[...] = _layer_norm(DN_ALPHA * h1_ref[...] + y, lg_ref[...], lb_ref[...])


def _peer_dense(h1T, wu, wvT, n_tab, c_tab, r2, e2, h1, lg, lb):
    T = h1T.shape[1]
    tm, eb = TM_PEER, EB_PEER
    col = lambda i, j: (0, i)
    col3 = lambda i, j: (0, 0, i)
    row = lambda i, j: (i, 0)
    return pl.pallas_call(
        functools.partial(_peer_dense_kernel, eb=eb), grid=(T // tm, PEER_EXPERTS // eb),
        in_specs=[pl.BlockSpec((D_MODEL, tm), col),
                  pl.BlockSpec((eb, D_MODEL), lambda i, j: (j, 0)),
                  pl.BlockSpec((D_MODEL, eb), lambda i, j: (0, j)),
                  pl.BlockSpec((PEER_HEADS, PEER_NKEYS, tm), col3), pl.BlockSpec((PEER_HEADS, PEER_NKEYS, tm), col3),
                  pl.BlockSpec((PEER_HEADS, PEER_NKEYS, tm), col3), pl.BlockSpec((PEER_HEADS, PEER_NKEYS, tm), col3),
                  pl.BlockSpec((tm, D_MODEL), row), _const_spec((1, D_MODEL)), _const_spec((1, D_MODEL))],
        out_specs=pl.BlockSpec((tm, D_MODEL), row),
        out_shape=jax.ShapeDtypeStruct((T, D_MODEL), F32),
        scratch_shapes=[pltpu.VMEM((D_MODEL, tm), F32)],
        compiler_params=_params(("parallel", "arbitrary"), INTERLEAVE_FLAGS), name="peer_dense",
    )(h1T, wu, wvT, n_tab, c_tab, r2, e2, h1, lg, lb)


def _pad_cols(w, lo, total):
    return jnp.pad(w, ((0, 0), (lo, total - lo - w.shape[1])))


def _prepare(S, w_in, b_in, q_norm_g, kv_norm_g, w_uq, w_uk, w_uv, w_o_attn, w_fourier, w_out, b_out, ln1_g, ln1_b,
             peer_w_q, peer_keys, peer_w_u, peer_w_v):
    o_kv, o_kr, o_z, o_g = Q_LORA, Q_LORA + KV_LORA, Q_LORA + KV_LORA + QK_ROPE_DIM, Q_LORA + KV_LORA + QK_ROPE_DIM + F_DIM
    row = lambda v: v.reshape(1, -1).astype(F32)
    w = {}
    w["wq"], w["bq"], w["gq"] = w_in[:, :o_kv].astype(BF16), row(b_in[:o_kv]), row(q_norm_g)
    w["wkv"], w["bkv"], w["gkv"] = w_in[:, o_kv:o_kr].astype(BF16), row(b_in[o_kv:o_kr]), row(kv_norm_g)
    wkr, bkr = w_in[:, o_kr:o_z], b_in[o_kr:o_z].reshape(1, -1)
    swap = lambda m: jnp.concatenate([m[:, ROPE_HALF:], m[:, :ROPE_HALF]], axis=1)
    w["wkr"], w["bkr"] = _pad_cols(wkr, QK_NOPE_DIM, HEAD_PAD).astype(BF16), _pad_cols(bkr, QK_NOPE_DIM, HEAD_PAD).astype(F32)
    w["wkrr"] = _pad_cols(swap(wkr), QK_NOPE_DIM, HEAD_PAD).astype(BF16)
    w["bkrr"] = _pad_cols(swap(bkr), QK_NOPE_DIM, HEAD_PAD).astype(F32)
    w["wz"], w["bz"] = w_in[:, o_z:o_g].astype(BF16), row(b_in[o_z:o_g])
    w["wg"], w["bg"] = w_in[:, o_g:].astype(BF16), row(b_in[o_g:])
    wq3 = w_uq.reshape(Q_LORA, N_HEADS, QK_DIM).transpose(1, 0, 2)
    rope3 = wq3[:, :, QK_NOPE_DIM:]
    rope3_sw = jnp.concatenate([rope3[:, :, ROPE_HALF:], rope3[:, :, :ROPE_HALF]], axis=2)
    padh = lambda m: jnp.pad(m, ((0, 0), (0, 0), (0, HEAD_PAD - m.shape[2])))
    w["wqu"] = padh(wq3).astype(BF16)
    w["wqr"] = padh(jnp.concatenate([jnp.zeros_like(wq3[:, :, :QK_NOPE_DIM]), rope3_sw], axis=2)).astype(BF16)
    wk3 = w_uk.reshape(KV_LORA, N_HEADS, QK_NOPE_DIM).transpose(1, 0, 2)
    w["wku"] = padh(wk3).astype(BF16)
    w["wvu"] = w_uv.astype(BF16)
    pos = jnp.arange(S, dtype=F32)
    inv = 1.0 / (ROPE_THETA ** (jnp.arange(0, QK_ROPE_DIM, 2, dtype=F32) / QK_ROPE_DIM))
    ang = pos[:, None] * inv[None, :]
    cos, sin = jnp.cos(ang), jnp.sin(ang)
    ones, zeros = jnp.ones((S, QK_NOPE_DIM), F32), jnp.zeros((S, QK_NOPE_DIM), F32)
    tail = jnp.zeros((S, HEAD_PAD - QK_DIM), F32)
    scale = QK_DIM ** -0.5 * math.log2(math.e)
    w["cos_q"] = jnp.concatenate([ones, cos, cos, tail], axis=1) * scale
    w["sin_q"] = jnp.concatenate([zeros, -sin, sin, tail], axis=1) * scale
    w["cos_k"] = jnp.concatenate([zeros, cos, cos, tail], axis=1)
    w["sin_k"] = jnp.concatenate([zeros, -sin, sin, tail], axis=1)
    n1 = S // LANES
    idx = lambda n: jnp.arange(n, dtype=jnp.int32)
    ang_c = (2.0 * math.pi / F_GROUP_DIM) * ((idx(F_GROUP_DIM)[:, None] * idx(F_GROUP_DIM)[None, :]) % F_GROUP_DIM).astype(F32)
    w["cs128"] = jnp.concatenate([jnp.cos(ang_c), jnp.sin(ang_c)], axis=1).astype(BF16)
    ang_1 = (2.0 * math.pi / n1) * ((idx(n1)[:, None] * idx(n1)[None, :]) % n1).astype(F32)
    c1, s1 = jnp.cos(ang_1), jnp.sin(ang_1)
    w["m1"] = jnp.concatenate([jnp.concatenate([c1, s1], axis=1), jnp.concatenate([-s1, c1], axis=1)], axis=0).astype(BF16)
    kk = idx(n1)[:, None, None] + n1 * idx(LANES)[None, :, None]
    ang_g = (2.0 * math.pi / S) * ((kk * idx(LANES)[None, None, :]) % S).astype(F32)
    w["gtab"] = jnp.concatenate([jnp.cos(ang_g), jnp.sin(ang_g)], axis=2).astype(BF16)
    w["wo"], w["wf"], w["wout"] = w_o_attn.astype(BF16), w_fourier.astype(BF16), w_out.astype(BF16)
    w["bout"], w["ln1_g"], w["ln1_b"] = row(b_out), row(ln1_g), row(ln1_b)
    w["wqT"] = peer_w_q.T.astype(BF16)
    w["keys"] = peer_keys.reshape(2 * PEER_HEADS, PEER_NKEYS, PEER_HALF).astype(BF16)
    w["wu"] = peer_w_u.astype(BF16)
    w["wvT"] = peer_w_v.T.astype(BF16)
    return w


def kernel(x, ln0_g, ln0_b, w_in, b_in, q_norm_g, kv_norm_g, w_uq, w_uk, w_uv, w_o_attn, w_fourier, w_out, b_out, ln1_g,
           ln1_b, peer_w_q, peer_keys, peer_w_u, peer_w_v, ln2_g, ln2_b):
    B, S, D = x.shape
    assert D == D_MODEL and w_in.shape[0] == DEPTH
    T = B * S
    n1 = S // LANES
    assert S % max(TM_PREP, TQ_ATTN, TM_POST, LANES * SUBLANES) == 0 and T % max(TM_LN, TM_PEER, TM_FG, TM_TOPK) == 0
    row = lambda v: v.reshape(1, -1).astype(F32)
    h = x.reshape(T, D)
    w = _prepare(S, w_in[0], b_in[0], q_norm_g[0], kv_norm_g[0], w_uq[0], w_uk[0], w_uv[0], w_o_attn[0], w_fourier[0],
                 w_out[0], b_out[0], ln1_g[0], ln1_b[0], peer_w_q[0], peer_keys[0], peer_w_u[0], peer_w_v[0])
    h0, h0b = _ln0(h, row(ln0_g), row(ln0_b))
    qT, k, vT = _mla_prep(h0b, B, S, w)
    oT = _attention(qT, k, vT)
    ur, ui, g = _fgate(h0b, w)
    y = _fft_a(ur.reshape(B, n1, LANES * F_DIM), ui.reshape(B, n1, LANES * F_DIM), w["m1"])
    yf = _fft_b(y.reshape(B, 2, n1, LANES, F_DIM), w["gtab"], 1.0 / math.sqrt(S * F_GROUP_DIM))
    h1, h1T = _post(oT, yf.reshape(T, F_DIM), g, h0, B, S, w)
    n_tab, c_tab, r2, e2 = _peer_topk(h1T, w["wqT"], w["keys"])
    out = _peer_dense(h1T, w["wu"], w["wvT"], n_tab, c_tab, r2, e2, h1, row(ln2_g[0]), row(ln2_b[0]))
    return out.reshape(B, S, D)
```

```python
import functools
import math

import jax
import jax.numpy as jnp
from jax import lax
from jax.experimental import pallas as pl
from jax.experimental.pallas import tpu as pltpu

F32 = jnp.float32
BF16 = jnp.bfloat16

D_MODEL = 1024
N_HEADS = 8
QK_NOPE_DIM = 64
QK_ROPE_DIM = 32
ROPE_HALF = QK_ROPE_DIM // 2
QK_DIM = QK_NOPE_DIM + QK_ROPE_DIM
V_DIM = 64
Q_LORA = 256
KV_LORA = 256
ROPE_THETA = 10000.0
F_GROUPS = 4
F_GROUP_DIM = 128
F_DIM = F_GROUPS * F_GROUP_DIM
PEER_HEADS = 8
PEER_NKEYS = 128
PEER_EXPERTS = PEER_NKEYS * PEER_NKEYS
PEER_HALF = 128
PEER_TOPK = 16
DEPTH = 1
DN_ALPHA = (2.0 * DEPTH) ** 0.25
LN_EPS = 1e-5
RMS_EPS = 1e-6

LANES = 128
SUBLANES = 8
HEAD_PAD = LANES
VMEM_LIMIT_BYTES = 56 * 1024 * 1024

TM_LN = 512
TM_PREP = 256
TQ_ATTN = 1024
TQ_SUB = 256
TK_ATTN = 512
TM_FG = 256
FFT_A_LANES = 4096
FFT_B_K1 = 8
TM_POST = 256
TM_TOPK = 256
TM_PEER = 512
EB_PEER = 2048
PEER_CHAIN = 256

_CAND_GROUPS = ((0, 0, 8), (0, 8, 8), (1, 0, 8), (2, 0, 5), (3, 0, 4), (4, 0, 3), (5, 0, 2), (6, 0, 2), (7, 0, 2))


def _params(sem):
    return pltpu.CompilerParams(dimension_semantics=sem, vmem_limit_bytes=VMEM_LIMIT_BYTES)


def _const_spec(shape):
    nd = len(shape)
    return pl.BlockSpec(shape, lambda *_: (0,) * nd)


def _layer_norm(x, g, b):
    mu = jnp.mean(x, axis=-1, keepdims=True)
    xc = x - mu
    var = jnp.mean(xc * xc, axis=-1, keepdims=True)
    return xc * lax.rsqrt(var + LN_EPS) * g + b


def _rms_norm(x, g):
    return x * lax.rsqrt(jnp.mean(x * x, axis=-1, keepdims=True) + RMS_EPS) * g


def _dot(a, b):
    return jnp.dot(a, b, preferred_element_type=F32)


def _ln0_kernel(x_ref, g_ref, b_ref, h_ref, hb_ref):
    h = _layer_norm(x_ref[...], g_ref[...], b_ref[...])
    h_ref[...] = h
    hb_ref[...] = h.astype(BF16)


def _ln0(x2, g, b):
    T = x2.shape[0]
    tm = TM_LN
    return pl.pallas_call(
        _ln0_kernel,
        grid=(T // tm,),
        in_specs=[pl.BlockSpec((tm, D_MODEL), lambda i: (i, 0)), _const_spec((1, D_MODEL)), _const_spec((1, D_MODEL))],
        out_specs=[pl.BlockSpec((tm, D_MODEL), lambda i: (i, 0)), pl.BlockSpec((tm, D_MODEL), lambda i: (i, 0))],
        out_shape=[jax.ShapeDtypeStruct((T, D_MODEL), F32), jax.ShapeDtypeStruct((T, D_MODEL), BF16)],
        compiler_params=_params(("parallel",)),
        name="ln0",
    )(x2, g, b)


def _mla_prep_kernel(hb_ref, wq_ref, bq_ref, gq_ref, wqu_ref, wqr_ref, wkv_ref, bkv_ref, gkv_ref, wku_ref, wvu_ref,
                     wkr_ref, bkr_ref, wkrr_ref, bkrr_ref, cq_ref, sq_ref, ck_ref, sk_ref,
                     qT_ref, k_ref, vT_ref):
    hb = hb_ref[...]
    cq = _rms_norm(_dot(hb, wq_ref[...]) + bq_ref[...], gq_ref[...]).astype(BF16)
    ckv = _rms_norm(_dot(hb, wkv_ref[...]) + bkv_ref[...], gkv_ref[...]).astype(BF16)
    cos_q, sin_q = cq_ref[...], sq_ref[...]
    k_rope = ((_dot(hb, wkr_ref[...]) + bkr_ref[...]) * ck_ref[...]
              + (_dot(hb, wkrr_ref[...]) + bkrr_ref[...]) * sk_ref[...])
    for h in range(N_HEADS):
        q_h = _dot(cq, wqu_ref[h]) * cos_q + _dot(cq, wqr_ref[h]) * sin_q
        qT_ref[0, h * HEAD_PAD:(h + 1) * HEAD_PAD, :] = q_h.T.astype(BF16)
        k_ref[0, h] = (_dot(ckv, wku_ref[h]) + k_rope).astype(BF16)
    v_all = _dot(ckv, wvu_ref[...])
    vT_ref[0] = v_all.T.astype(BF16)


def _mla_prep(hb, B, S, w):
    tm = TM_PREP
    nb = S // tm
    tok = lambda b, i: (b * nb + i, 0)
    pos = lambda b, i: (i, 0)
    in_specs = [
        pl.BlockSpec((tm, D_MODEL), tok),
        _const_spec((D_MODEL, Q_LORA)), _const_spec((1, Q_LORA)), _const_spec((1, Q_LORA)),
        _const_spec((N_HEADS, Q_LORA, HEAD_PAD)), _const_spec((N_HEADS, Q_LORA, HEAD_PAD)),
        _const_spec((D_MODEL, KV_LORA)), _const_spec((1, KV_LORA)), _const_spec((1, KV_LORA)),
        _const_spec((N_HEADS, KV_LORA, HEAD_PAD)), _const_spec((KV_LORA, N_HEADS * V_DIM)),
        _const_spec((D_MODEL, HEAD_PAD)), _const_spec((1, HEAD_PAD)),
        _const_spec((D_MODEL, HEAD_PAD)), _const_spec((1, HEAD_PAD)),
        pl.BlockSpec((tm, HEAD_PAD), pos), pl.BlockSpec((tm, HEAD_PAD), pos),
        pl.BlockSpec((tm, HEAD_PAD), pos), pl.BlockSpec((tm, HEAD_PAD), pos),
    ]
    out_specs = [
        pl.BlockSpec((1, N_HEADS * HEAD_PAD, tm), lambda b, i: (b, 0, i)),
        pl.BlockSpec((1, N_HEADS, tm, HEAD_PAD), lambda b, i: (b, 0, i, 0)),
        pl.BlockSpec((1, N_HEADS * V_DIM, tm), lambda b, i: (b, 0, i)),
    ]
    out_shape = [
        jax.ShapeDtypeStruct((B, N_HEADS * HEAD_PAD, S), BF16),
        jax.ShapeDtypeStruct((B, N_HEADS, S, HEAD_PAD), BF16),
        jax.ShapeDtypeStruct((B, N_HEADS * V_DIM, S), BF16),
    ]
    return pl.pallas_call(
        _mla_prep_kernel, grid=(B, nb), in_specs=in_specs, out_specs=out_specs, out_shape=out_shape,
        compiler_params=_params(("parallel", "parallel")), name="mla_prep",
    )(hb, w["wq"], w["bq"], w["gq"], w["wqu"], w["wqr"], w["wkv"], w["bkv"], w["gkv"], w["wku"], w["wvu"],
      w["wkr"], w["bkr"], w["wkrr"], w["bkrr"], w["cos_q"], w["sin_q"], w["cos_k"], w["sin_k"])


def _col_reduce(x, op, final):
    parts = [x[i:i + SUBLANES] for i in range(0, x.shape[0], SUBLANES)]
    while len(parts) > 1:
        parts = [op(parts[i], parts[i + 1]) for i in range(0, len(parts), 2)]
    return final(parts[0], axis=0, keepdims=True)


def _attn_kernel(qT_ref, k_ref, vT_ref, oT_ref, s_ref, *, tk, n_sub):
    tq = qT_ref.shape[2]
    tsub = tq // n_sub
    n_chunks = k_ref.shape[2] // tk
    subs = range(n_sub)

    def scores(c, slot):
        off = pl.multiple_of(c * tk, tk)
        k_c = k_ref[0, 0, pl.ds(off, tk), :]
        cmax = []
        for u in subs:
            s = _dot(k_c, qT_ref[0, :, u * tsub:(u + 1) * tsub])
            s_ref[slot, u] = s
            cmax.append(_col_reduce(s, jnp.maximum, jnp.max))
        return tuple(cmax)

    def softmax_pv(c, slot, carry, cmax):
        off = pl.multiple_of(c * tk, tk)
        vT_c = vT_ref[0, :, pl.ds(off, tk)]
        m_new = [jnp.maximum(carry[u][0], cmax[u]) for u in subs]
        p = [jnp.exp2(s_ref[slot, u] - m_new[u]) for u in subs]
        alpha = [jnp.exp2(carry[u][0] - m_new[u]) for u in subs]
        l = [alpha[u] * carry[u][1] + _col_reduce(p[u], jnp.add, jnp.sum) for u in subs]
        acc = [alpha[u] * carry[u][2] + _dot(vT_c, p[u].astype(BF16)) for u in subs]
        return tuple((m_new[u], l[u], acc[u]) for u in subs)

    def body(i, carry):
        state, cmax0 = carry
        c0 = 2 * i
        cmax1 = scores(c0 + 1, 1)
        state = softmax_pv(c0, 0, state, cmax0)
        cmax0 = scores(jnp.minimum(c0 + 2, n_chunks - 1), 0)
        return softmax_pv(c0 + 1, 1, state, cmax1), cmax0

    cmax_first = scores(0, 0)
    init = tuple((jnp.full((1, tsub), -jnp.inf, F32), jnp.zeros((1, tsub), F32), jnp.zeros((V_DIM, tsub), F32))
                 for _ in range(n_sub))
    fin, _ = lax.fori_loop(0, n_chunks // 2, body, (init, cmax_first))
    for u in range(n_sub):
        _, l, acc = fin[u]
        oT_ref[0, :, u * tsub:(u + 1) * tsub] = acc * (1.0 / l)


def _attention(qT, k, vT):
    B, _, S = qT.shape
    tq = TQ_ATTN
    return pl.pallas_call(
        functools.partial(_attn_kernel, tk=min(TK_ATTN, S), n_sub=TQ_ATTN // TQ_SUB),
        grid=(B, N_HEADS, S // tq),
        in_specs=[
            pl.BlockSpec((1, HEAD_PAD, tq), lambda b, h, i: (b, h, i)),
            pl.BlockSpec((1, 1, S, HEAD_PAD), lambda b, h, i: (b, h, 0, 0)),
            pl.BlockSpec((1, V_DIM, S), lambda b, h, i: (b, h, 0)),
        ],
        out_specs=pl.BlockSpec((1, V_DIM, tq), lambda b, h, i: (b, h, i)),
        out_shape=jax.ShapeDtypeStruct((B, N_HEADS * V_DIM, S), F32),
        scratch_shapes=[pltpu.VMEM((2, TQ_ATTN // TQ_SUB, min(TK_ATTN, S), TQ_SUB), F32)],
        compiler_params=_params(("parallel", "parallel", "parallel")),
        name="attention",
    )(qT, k, vT)


def _fgate_kernel(hb_ref, wz_ref, bz_ref, cs_ref, wg_ref, bg_ref, ur_ref, ui_ref, g_ref):
    hb = hb_ref[...]
    z = (_dot(hb, wz_ref[...]) + bz_ref[...]).astype(BF16)
    cs = cs_ref[...]
    for gi in range(F_GROUPS):
        lo, hi = gi * F_GROUP_DIM, (gi + 1) * F_GROUP_DIM
        pq = _dot(z[:, lo:hi], cs)
        ur_ref[:, lo:hi] = pq[:, :F_GROUP_DIM].astype(BF16)
        ui_ref[:, lo:hi] = (-pq[:, F_GROUP_DIM:]).astype(BF16)
    g_ref[...] = jax.nn.sigmoid(_dot(hb, wg_ref[...]) + bg_ref[...]).astype(BF16)


def _fgate(hb, w):
    T = hb.shape[0]
    tm = TM_FG
    row = lambda i: (i, 0)
    return pl.pallas_call(
        _fgate_kernel, grid=(T // tm,),
        in_specs=[pl.BlockSpec((tm, D_MODEL), row), _const_spec((D_MODEL, F_DIM)), _const_spec((1, F_DIM)),
                  _const_spec((F_GROUP_DIM, 2 * F_GROUP_DIM)), _const_spec((D_MODEL, 2 * D_MODEL)),
                  _const_spec((1, 2 * D_MODEL))],
        out_specs=[pl.BlockSpec((tm, F_DIM), row), pl.BlockSpec((tm, F_DIM), row), pl.BlockSpec((tm, 2 * D_MODEL), row)],
        out_shape=[jax.ShapeDtypeStruct((T, F_DIM), BF16), jax.ShapeDtypeStruct((T, F_DIM), BF16),
                   jax.ShapeDtypeStruct((T, 2 * D_MODEL), BF16)],
        compiler_params=_params(("parallel",)), name="fgate",
    )(hb, w["wz"], w["bz"], w["cs128"], w["wg"], w["bg"])


def _fft_a_kernel(ur_ref, ui_ref, m1_ref, y_ref):
    u = jnp.concatenate([ur_ref[0], ui_ref[0]], axis=0)
    y_ref[0] = _dot(m1_ref[...], u).astype(BF16)


def _fft_a(ur3, ui3, m1):
    B, n1, W = ur3.shape
    L = min(FFT_A_LANES, W)
    blk = lambda b, j: (b, 0, j)
    return pl.pallas_call(
        _fft_a_kernel, grid=(B, W // L),
        in_specs=[pl.BlockSpec((1, n1, L), blk), pl.BlockSpec((1, n1, L), blk), _const_spec((2 * n1, 2 * n1))],
        out_specs=pl.BlockSpec((1, 2 * n1, L), blk),
        out_shape=jax.ShapeDtypeStruct((B, 2 * n1, W), BF16),
        compiler_params=_params(("parallel", "parallel")), name="fft_a",
    )(ur3, ui3, m1)


def _fft_b_kernel(y_ref, g_ref, o_ref, *, nk, scale):
    for j in range(nk):
        ycat = jnp.concatenate([y_ref[0, 0, j], y_ref[0, 1, j]], axis=0)
        o_ref[0, :, j * F_DIM:(j + 1) * F_DIM] = (_dot(g_ref[j], ycat) * scale).astype(BF16)


def _fft_b(y5, gtab, scale):
    B, _, n1, _, _ = y5.shape
    nk = min(FFT_B_K1, n1)
    return pl.pallas_call(
        functools.partial(_fft_b_kernel, nk=nk, scale=scale), grid=(B, n1 // nk),
        in_specs=[pl.BlockSpec((1, 2, nk, LANES, F_DIM), lambda b, j: (b, 0, j, 0, 0)),
                  pl.BlockSpec((nk, LANES, 2 * LANES), lambda b, j: (j, 0, 0))],
        out_specs=pl.BlockSpec((1, LANES, nk * F_DIM), lambda b, j: (b, 0, j)),
        out_shape=jax.ShapeDtypeStruct((B, LANES, n1 * F_DIM), BF16),
        compiler_params=_params(("parallel", "parallel")), name="fft_b",
    )(y5, gtab)


def _post_kernel(oT_ref, yf_ref, g_ref, h0_ref, wo_ref, wf_ref, wout_ref, bout_ref, lg_ref, lb_ref, h1_ref, h1T_ref):
    o = oT_ref[0].T.astype(BF16)
    y_a = _dot(o, wo_ref[...])
    y_f = _dot(yf_ref[...], wf_ref[...])
    g = g_ref[...].astype(F32)
    m = (g[:, :D_MODEL] * y_a + g[:, D_MODEL:] * y_f).astype(BF16)
    mix = _dot(m, wout_ref[...]) + bout_ref[...]
    h1 = _layer_norm(DN_ALPHA * h0_ref[...] + mix, lg_ref[...], lb_ref[...])
    h1_ref[...] = h1
    h1T_ref[...] = h1.T.astype(BF16)


def _post(oT, yf, g, h0, B, S, w):
    T = B * S
    tm = TM_POST
    nb = S // tm
    tok = lambda b, i: (b * nb + i, 0)
    return pl.pallas_call(
        _post_kernel, grid=(B, nb),
        in_specs=[pl.BlockSpec((1, N_HEADS * V_DIM, tm), lambda b, i: (b, 0, i)),
                  pl.BlockSpec((tm, F_DIM), tok), pl.BlockSpec((tm, 2 * D_MODEL), tok), pl.BlockSpec((tm, D_MODEL), tok),
                  _const_spec((N_HEADS * V_DIM, D_MODEL)), _const_spec((F_DIM, D_MODEL)), _const_spec((D_MODEL, D_MODEL)),
                  _const_spec((1, D_MODEL)), _const_spec((1, D_MODEL)), _const_spec((1, D_MODEL))],
        out_specs=[pl.BlockSpec((tm, D_MODEL), tok), pl.BlockSpec((D_MODEL, tm), lambda b, i: (0, b * nb + i))],
        out_shape=[jax.ShapeDtypeStruct((T, D_MODEL), F32), jax.ShapeDtypeStruct((D_MODEL, T), BF16)],
        compiler_params=_params(("parallel", "parallel")), name="post",
    )(oT, yf, g, h0, w["wo"], w["wf"], w["wout"], w["bout"], w["ln1_g"], w["ln1_b"])


def _extract_top16(s):
    row = lax.broadcasted_iota(jnp.int32, s.shape, 0).astype(F32)
    slot = lax.broadcasted_iota(jnp.int32, (PEER_TOPK, s.shape[1]), 0)
    rank = jnp.full(s.shape, float(PEER_TOPK), F32)
    vals = jnp.zeros((PEER_TOPK, s.shape[1]), F32)
    for r in range(PEER_TOPK):
        m = jnp.max(s, axis=0, keepdims=True)
        idx = jnp.min(jnp.where(s == m, row, float(PEER_NKEYS)), axis=0, keepdims=True)
        hit = row == idx
        s = jnp.where(hit, -jnp.inf, s)
        rank = jnp.where(hit, float(r), rank)
        vals = jnp.where(slot == r, m, vals)
    return vals, rank


def _cand_rows():
    groups = [[(i, j0 + r) if r < valid else None for r in range(SUBLANES)] for (i, j0, valid) in _CAND_GROUPS]
    groups.append([(SUBLANES + r, 0) for r in range(SUBLANES)])
    return groups


def _static_beats(cp, c):
    if cp == c:
        return 0
    if cp[0] <= c[0] and cp[1] <= c[1]:
        return 1
    if cp[0] >= c[0] and cp[1] >= c[1]:
        return 0
    return None


def _static_counts():
    rows = _cand_rows()
    out = [[0.0] * SUBLANES for _ in rows]
    for cg in rows:
        for cp in cg:
            if cp is None:
                continue
            for g, tgt in enumerate(rows):
                res = [_static_beats(cp, c) if c is not None else 0 for c in tgt]
                if all(v is not None for v in res):
                    for r in range(SUBLANES):
                        out[g][r] += float(res[r])
    return out


def _select_pairs(a_rep, b_lo, b_hi, a_hi, static_counts):
    L = b_lo.shape[1]
    sub = lax.broadcasted_iota(jnp.int32, (SUBLANES, L), 0)
    rows = _cand_rows()
    groups = []
    for (i, j0, valid) in _CAND_GROUPS:
        v = a_rep[i] + (b_lo if j0 == 0 else b_hi)
        if valid < SUBLANES:
            v = jnp.where(sub < valid, v, -jnp.inf)
        groups.append(v)
    groups.append(a_hi + jnp.broadcast_to(b_lo[0:1, :], (SUBLANES, L)))
    n_groups = len(groups)
    counts = [static_counts[g] for g in range(n_groups)]
    for gp in range(n_groups):
        for rp in range(SUBLANES):
            cp = rows[gp][rp]
            if cp is None:
                continue
            vb = jnp.broadcast_to(groups[gp][rp:rp + 1, :], (SUBLANES, L))
            for g in range(n_groups):
                if all(c is None or _static_beats(cp, c) is not None for c in rows[g]):
                    continue
                if g < gp:
                    beats = jnp.where(vb > groups[g], 1.0, 0.0)
                elif g > gp:
                    beats = jnp.where(vb >= groups[g], 1.0, 0.0)
                else:
                    beats = jnp.where(sub > rp, jnp.where(vb >= groups[g], 1.0, 0.0), jnp.where(vb > groups[g], 1.0, 0.0))
                counts[g] = counts[g] + beats
    top = groups[0][0:1, :]
    valid_rows = [g[2] for g in _CAND_GROUPS] + [SUBLANES]
    sel = []
    z = jnp.zeros((1, L), F32)
    for g in range(n_groups):
        s_g = jnp.where(counts[g] < float(PEER_TOPK), 1.0, 0.0)
        if valid_rows[g] < SUBLANES:
            s_g = jnp.where(sub < valid_rows[g], s_g, 0.0)
        sel.append(s_g)
        z = z + jnp.sum(s_g * jnp.exp(groups[g] - top), axis=0, keepdims=True)
    n = [jnp.sum(sel[0] + sel[1], axis=0, keepdims=True)]
    for g in range(2, n_groups - 1):
        n.append(jnp.sum(sel[g], axis=0, keepdims=True))
    for r in range(SUBLANES):
        n.append(sel[n_groups - 1][r:r + 1, :])
    return n, z


def _rows_from_rep(rep, lo):
    sub = lax.broadcasted_iota(jnp.int32, rep[0].shape, 0)
    out = rep[lo]
    for r in range(1, SUBLANES):
        out = jnp.where(sub == r, rep[lo + r], out)
    return out


def _oddeven_merge_pairs(n):
    pairs = []
    t = n.bit_length() - 1
    for pi in range(t):
        p = 1 << pi
        for ki in range(pi, -1, -1):
            k = 1 << ki
            for j in range(k % p, n - k, 2 * k):
                for i in range(min(k, n - j - k)):
                    if (i + j) // (2 * p) == (i + j + k) // (2 * p):
                        pairs.append((i + j, i + j + k))
    return pairs


_SORT16 = _oddeven_merge_pairs(PEER_TOPK)


def _top16_values(s):
    w = [s[g * SUBLANES:(g + 1) * SUBLANES] for g in range(PEER_NKEYS // SUBLANES)]
    for (i, j) in _SORT16:
        w[i], w[j] = jnp.maximum(w[i], w[j]), jnp.minimum(w[i], w[j])
    for shift in (4, 2, 1):
        other = [pltpu.roll(x, shift, axis=0) for x in w]
        w = [jnp.maximum(w[i], other[PEER_TOPK - 1 - i]) for i in range(PEER_TOPK)]
        d = PEER_TOPK // 2
        while d >= 1:
            for i in range(PEER_TOPK):
                if i & d == 0:
                    w[i], w[i + d] = jnp.maximum(w[i], w[i + d]), jnp.minimum(w[i], w[i + d])
            d //= 2
    return w


def _tie_flags(s, w):
    flag = jnp.zeros_like(w[0])
    for r in range(PEER_TOPK - 1):
        flag = flag + jnp.where(w[r] == w[r + 1], 1.0, 0.0)
    parts = [jnp.where(s[g * SUBLANES:(g + 1) * SUBLANES] >= w[PEER_TOPK - 1], 1.0, 0.0)
             for g in range(PEER_NKEYS // SUBLANES)]
    while len(parts) > 1:
        parts = [parts[i] + parts[i + 1] for i in range(0, len(parts), 2)]
    count = jnp.sum(parts[0], axis=0, keepdims=True)
    return flag + (count - float(PEER_TOPK))


def _peer_topk_kernel(h1T_ref, wqT_ref, keys_ref, sc_ref, n_ref, c_ref, r2_ref, e2_ref, qp_ref):
    tm = h1T_ref.shape[1]
    qp_ref[...] = _dot(wqT_ref[...], h1T_ref[...]).astype(BF16)

    def scores(h, lo):
        r0, r1 = 2 * h * PEER_HALF, (2 * h + 1) * PEER_HALF
        if not isinstance(h, int):
            r0, r1 = pl.multiple_of(r0, PEER_HALF), pl.multiple_of(r1, PEER_HALF)
        s1 = _dot(keys_ref[2 * h], qp_ref[pl.ds(r0, PEER_HALF), pl.ds(lo, LANES)])
        s2 = _dot(keys_ref[2 * h + 1], qp_ref[pl.ds(r1, PEER_HALF), pl.ds(lo, LANES)])
        return s1, s2

    def store(h, lo, n_a, c_a, rank2, e2):
        n_ref[h, :, pl.ds(lo, LANES)] = n_a
        c_ref[h, :, pl.ds(lo, LANES)] = c_a
        r2_ref[h, :, pl.ds(lo, LANES)] = rank2.astype(BF16)
        e2_ref[h, :, pl.ds(lo, LANES)] = e2.astype(BF16)

    def fast_head(h, lo):
        s1, s2 = scores(h, lo)
        w1, w2 = _top16_values(s1), _top16_values(s2)
        n, z = _select_pairs(w1, _rows_from_rep(w2, 0), _rows_from_rep(w2, SUBLANES), _rows_from_rep(w1, SUBLANES),
                             sc_ref)
        n_parts, r_parts = [], []
        for g in range(PEER_NKEYS // SUBLANES):
            s1_g, s2_g = s1[g * SUBLANES:(g + 1) * SUBLANES], s2[g * SUBLANES:(g + 1) * SUBLANES]
            n_g = jnp.zeros_like(s1_g)
            r_g = jnp.full_like(s2_g, float(PEER_TOPK))
            for r in range(PEER_TOPK):
                n_g = jnp.where(s1_g == w1[r], n[r], n_g)
                r_g = jnp.where(s2_g == w2[r], float(r), r_g)
            n_parts.append(n_g)
            r_parts.append(r_g)
        c_a = jnp.exp(s1 - w1[0][0:1, :]) * (1.0 / z)
        e2 = jnp.exp(s2 - w2[0][0:1, :])
        store(h, lo, jnp.concatenate(n_parts, axis=0), c_a, jnp.concatenate(r_parts, axis=0), e2)
        return _tie_flags(s1, w1) + _tie_flags(s2, w2)

    def exact_head(h, lo):
        s1, s2 = scores(h, lo)
        a_vals, rank1 = _extract_top16(s1)
        b_vals, rank2 = _extract_top16(s2)
        a_rep = [jnp.broadcast_to(a_vals[i:i + 1, :], (SUBLANES, LANES)) for i in range(PEER_TOPK)]
        n, z = _select_pairs(a_rep, b_vals[:SUBLANES], b_vals[SUBLANES:], a_vals[SUBLANES:], sc_ref)
        n_a = jnp.zeros_like(s1)
        for i in range(PEER_TOPK):
            n_a = n_a + jnp.where(rank1 == float(i), n[i], 0.0)
        c_a = jnp.exp(s1 - a_vals[0:1, :]) * (1.0 / z)
        e2 = jnp.exp(s2 - b_vals[0:1, :])
        store(h, lo, n_a, c_a, rank2, e2)

    def chunk(ci, _):
        lo = pl.multiple_of(ci * LANES, LANES)
        flags = jnp.zeros((SUBLANES, LANES), F32)
        for h in range(PEER_HEADS):
            flags = flags + fast_head(h, lo)

        @pl.when(jnp.max(flags) > 0.0)
        def _():
            def per_head(h, carry):
                exact_head(h, lo)
                return carry
            lax.fori_loop(0, PEER_HEADS, per_head, 0)
        return 0

    lax.fori_loop(0, tm // LANES, chunk, 0)


def _peer_topk(h1T, wqT, keys):
    T = h1T.shape[1]
    tm = TM_TOPK
    nq = wqT.shape[0]
    col = lambda i: (0, i)
    col3 = lambda i: (0, 0, i)
    static = jnp.broadcast_to(jnp.asarray(_static_counts(), F32)[:, :, None], (len(_CAND_GROUPS) + 1, SUBLANES, LANES))
    return pl.pallas_call(
        _peer_topk_kernel, grid=(T // tm,),
        in_specs=[pl.BlockSpec((D_MODEL, tm), col), _const_spec((nq, D_MODEL)),
                  _const_spec((2 * PEER_HEADS, PEER_NKEYS, PEER_HALF)), _const_spec(static.shape)],
        out_specs=[pl.BlockSpec((PEER_HEADS, PEER_NKEYS, tm), col3), pl.BlockSpec((PEER_HEADS, PEER_NKEYS, tm), col3),
                   pl.BlockSpec((PEER_HEADS, PEER_NKEYS, tm), col3), pl.BlockSpec((PEER_HEADS, PEER_NKEYS, tm), col3)],
        out_shape=[jax.ShapeDtypeStruct((PEER_HEADS, PEER_NKEYS, T), F32),
                   jax.ShapeDtypeStruct((PEER_HEADS, PEER_NKEYS, T), F32),
                   jax.ShapeDtypeStruct((PEER_HEADS, PEER_NKEYS, T), BF16),
                   jax.ShapeDtypeStruct((PEER_HEADS, PEER_NKEYS, T), BF16)],
        scratch_shapes=[pltpu.VMEM((nq, tm), BF16)],
        compiler_params=_params(("parallel",)), name="peer_topk",
    )(h1T, wqT, keys, static)


def _peer_dense_kernel(xT_ref, wu_ref, wvT_ref, n_ref, c_ref, r2_ref, e2_ref, h1_ref, lg_ref, lb_ref, out_ref,
                       acc_ref, *, eb):
    j = pl.program_id(1)
    tm = xT_ref.shape[1]

    @pl.when(j == 0)
    def _():
        acc_ref[...] = jnp.zeros_like(acc_ref)

    n_chains = eb // PEER_CHAIN
    rows = [slice(ch * PEER_CHAIN, (ch + 1) * PEER_CHAIN) for ch in range(n_chains)]

    def first_matmul(ch):
        return _dot(wu_ref[rows[ch], :], xT_ref[...])

    def gate_gelu(ch, hpre):
        acts = []
        for aa in range(PEER_CHAIN // PEER_NKEYS):
            a_key = (j * n_chains + ch) * (PEER_CHAIN // PEER_NKEYS) + aa
            gate = jnp.zeros((PEER_NKEYS, tm), BF16)
            for h in range(PEER_HEADS):
                n_b = jnp.broadcast_to(n_ref[h, pl.ds(a_key, 1), :], (PEER_NKEYS, tm)).astype(BF16)
                c_b = jnp.broadcast_to(c_ref[h, pl.ds(a_key, 1), :], (PEER_NKEYS, tm)).astype(BF16)
                gate = gate + jnp.where(r2_ref[h] < n_b, e2_ref[h], jnp.zeros((), BF16)) * c_b
            hp = hpre[aa * PEER_NKEYS:(aa + 1) * PEER_NKEYS, :]
            act = 0.5 * hp * (1.0 + lax.erf(hp * (1.0 / math.sqrt(2.0))))
            acts.append(act.astype(BF16) * gate)
        return jnp.concatenate(acts, axis=0)

    hpre, act = {}, {}
    for t in range(n_chains + 2):
        if t < n_chains:
            hpre[t] = first_matmul(t)
        if 0 <= t - 1 < n_chains:
            act[t - 1] = gate_gelu(t - 1, hpre.pop(t - 1))
        if 0 <= t - 2 < n_chains:
            acc_ref[...] += _dot(wvT_ref[:, rows[t - 2]], act.pop(t - 2))

    @pl.when(j == pl.num_programs(1) - 1)
    def _():
        y = acc_ref[...].T
        out_ref[...] = _layer_norm(DN_ALPHA * h1_ref[...] + y, lg_ref[...], lb_ref[...])


def _peer_dense(h1T, wu, wvT, n_tab, c_tab, r2, e2, h1, lg, lb):
    T = h1T.shape[1]
    tm, eb = TM_PEER, EB_PEER
    col = lambda i, j: (0, i)
    col3 = lambda i, j: (0, 0, i)
    row = lambda i, j: (i, 0)
    return pl.pallas_call(
        functools.partial(_peer_dense_kernel, eb=eb), grid=(T // tm, PEER_EXPERTS // eb),
        in_specs=[pl.BlockSpec((D_MODEL, tm), col),
                  pl.BlockSpec((eb, D_MODEL), lambda i, j: (j, 0)),
                  pl.BlockSpec((D_MODEL, eb), lambda i, j: (0, j)),
                  pl.BlockSpec((PEER_HEADS, PEER_NKEYS, tm), col3), pl.BlockSpec((PEER_HEADS, PEER_NKEYS, tm), col3),
                  pl.BlockSpec((PEER_HEADS, PEER_NKEYS, tm), col3), pl.BlockSpec((PEER_HEADS, PEER_NKEYS, tm), col3),
                  pl.BlockSpec((tm, D_MODEL), row), _const_spec((1, D_MODEL)), _const_spec((1, D_MODEL))],
        out_specs=pl.BlockSpec((tm, D_MODEL), row),
        out_shape=jax.ShapeDtypeStruct((T, D_MODEL), F32),
        scratch_shapes=[pltpu.VMEM((D_MODEL, tm), F32)],
        compiler_params=_params(("parallel", "arbitrary")), name="peer_dense",
    )(h1T, wu, wvT, n_tab, c_tab, r2, e2, h1, lg, lb)


def _pad_cols(w, lo, total):
    return jnp.pad(w, ((0, 0), (lo, total - lo - w.shape[1])))


def _prepare(S, w_in, b_in, q_norm_g, kv_norm_g, w_uq, w_uk, w_uv, w_o_attn, w_fourier, w_out, b_out, ln1_g, ln1_b,
             peer_w_q, peer_keys, peer_w_u, peer_w_v):
    o_kv, o_kr, o_z, o_g = Q_LORA, Q_LORA + KV_LORA, Q_LORA + KV_LORA + QK_ROPE_DIM, Q_LORA + KV_LORA + QK_ROPE_DIM + F_DIM
    row = lambda v: v.reshape(1, -1).astype(F32)
    w = {}
    w["wq"], w["bq"], w["gq"] = w_in[:, :o_kv].astype(BF16), row(b_in[:o_kv]), row(q_norm_g)
    w["wkv"], w["bkv"], w["gkv"] = w_in[:, o_kv:o_kr].astype(BF16), row(b_in[o_kv:o_kr]), row(kv_norm_g)
    wkr, bkr = w_in[:, o_kr:o_z], b_in[o_kr:o_z].reshape(1, -1)
    swap = lambda m: jnp.concatenate([m[:, ROPE_HALF:], m[:, :ROPE_HALF]], axis=1)
    w["wkr"], w["bkr"] = _pad_cols(wkr, QK_NOPE_DIM, HEAD_PAD).astype(BF16), _pad_cols(bkr, QK_NOPE_DIM, HEAD_PAD).astype(F32)
    w["wkrr"] = _pad_cols(swap(wkr), QK_NOPE_DIM, HEAD_PAD).astype(BF16)
    w["bkrr"] = _pad_cols(swap(bkr), QK_NOPE_DIM, HEAD_PAD).astype(F32)
    w["wz"], w["bz"] = w_in[:, o_z:o_g].astype(BF16), row(b_in[o_z:o_g])
    w["wg"], w["bg"] = w_in[:, o_g:].astype(BF16), row(b_in[o_g:])
    wq3 = w_uq.reshape(Q_LORA, N_HEADS, QK_DIM).transpose(1, 0, 2)
    rope3 = wq3[:, :, QK_NOPE_DIM:]
    rope3_sw = jnp.concatenate([rope3[:, :, ROPE_HALF:], rope3[:, :, :ROPE_HALF]], axis=2)
    padh = lambda m: jnp.pad(m, ((0, 0), (0, 0), (0, HEAD_PAD - m.shape[2])))
    w["wqu"] = padh(wq3).astype(BF16)
    w["wqr"] = padh(jnp.concatenate([jnp.zeros_like(wq3[:, :, :QK_NOPE_DIM]), rope3_sw], axis=2)).astype(BF16)
    wk3 = w_uk.reshape(KV_LORA, N_HEADS, QK_NOPE_DIM).transpose(1, 0, 2)
    w["wku"] = padh(wk3).astype(BF16)
    w["wvu"] = w_uv.astype(BF16)
    pos = jnp.arange(S, dtype=F32)
    inv = 1.0 / (ROPE_THETA ** (jnp.arange(0, QK_ROPE_DIM, 2, dtype=F32) / QK_ROPE_DIM))
    ang = pos[:, None] * inv[None, :]
    cos, sin = jnp.cos(ang), jnp.sin(ang)
    ones, zeros = jnp.ones((S, QK_NOPE_DIM), F32), jnp.zeros((S, QK_NOPE_DIM), F32)
    tail = jnp.zeros((S, HEAD_PAD - QK_DIM), F32)
    scale = QK_DIM ** -0.5 * math.log2(math.e)
    w["cos_q"] = jnp.concatenate([ones, cos, cos, tail], axis=1) * scale
    w["sin_q"] = jnp.concatenate([zeros, -sin, sin, tail], axis=1) * scale
    w["cos_k"] = jnp.concatenate([zeros, cos, cos, tail], axis=1)
    w["sin_k"] = jnp.concatenate([zeros, -sin, sin, tail], axis=1)
    n1 = S // LANES
    idx = lambda n: jnp.arange(n, dtype=jnp.int32)
    ang_c = (2.0 * math.pi / F_GROUP_DIM) * ((idx(F_GROUP_DIM)[:, None] * idx(F_GROUP_DIM)[None, :]) % F_GROUP_DIM).astype(F32)
    w["cs128"] = jnp.concatenate([jnp.cos(ang_c), jnp.sin(ang_c)], axis=1).astype(BF16)
    ang_1 = (2.0 * math.pi / n1) * ((idx(n1)[:, None] * idx(n1)[None, :]) % n1).astype(F32)
    c1, s1 = jnp.cos(ang_1), jnp.sin(ang_1)
    w["m1"] = jnp.concatenate([jnp.concatenate([c1, s1], axis=1), jnp.concatenate([-s1, c1], axis=1)], axis=0).astype(BF16)
    kk = idx(n1)[:, None, None] + n1 * idx(LANES)[None, :, None]
    ang_g = (2.0 * math.pi / S) * ((kk * idx(LANES)[None, None, :]) % S).astype(F32)
    w["gtab"] = jnp.concatenate([jnp.cos(ang_g), jnp.sin(ang_g)], axis=2).astype(BF16)
    w["wo"], w["wf"], w["wout"] = w_o_attn.astype(BF16), w_fourier.astype(BF16), w_out.astype(BF16)
    w["bout"], w["ln1_g"], w["ln1_b"] = row(b_out), row(ln1_g), row(ln1_b)
    w["wqT"] = peer_w_q.T.astype(BF16)
    w["keys"] = peer_keys.reshape(2 * PEER_HEADS, PEER_NKEYS, PEER_HALF).astype(BF16)
    w["wu"] = peer_w_u.astype(BF16)
    w["wvT"] = peer_w_v.T.astype(BF16)
    return w


def kernel(x, ln0_g, ln0_b, w_in, b_in, q_norm_g, kv_norm_g, w_uq, w_uk, w_uv, w_o_attn, w_fourier, w_out, b_out, ln1_g,
           ln1_b, peer_w_q, peer_keys, peer_w_u, peer_w_v, ln2_g, ln2_b):
    B, S, D = x.shape
    assert D == D_MODEL and w_in.shape[0] == DEPTH
    T = B * S
    n1 = S // LANES
    assert S % max(TM_PREP, TQ_ATTN, TM_POST, LANES * SUBLANES) == 0 and T % max(TM_LN, TM_PEER, TM_FG, TM_TOPK) == 0
    row = lambda v: v.reshape(1, -1).astype(F32)
    h = x.reshape(T, D)
    w = _prepare(S, w_in[0], b_in[0], q_norm_g[0], kv_norm_g[0], w_uq[0], w_uk[0], w_uv[0], w_o_attn[0], w_fourier[0],
                 w_out[0], b_out[0], ln1_g[0], ln1_b[0], peer_w_q[0], peer_keys[0], peer_w_u[0], peer_w_v[0])
    h0, h0b = _ln0(h, row(ln0_g), row(ln0_b))
    qT, k, vT = _mla_prep(h0b, B, S, w)
    oT = _attention(qT, k, vT)
    ur, ui, g = _fgate(h0b, w)
    y = _fft_a(ur.reshape(B, n1, LANES * F_DIM), ui.reshape(B, n1, LANES * F_DIM), w["m1"])
    yf = _fft_b(y.reshape(B, 2, n1, LANES, F_DIM), w["gtab"], 1.0 / math.sqrt(S * F_GROUP_DIM))
    h1, h1T = _post(oT, yf.reshape(T, F_DIM), g, h0, B, S, w)
    n_tab, c_tab, r2, e2 = _peer_topk(h1T, w["wqT"], w["keys"])
    out = _peer_dense(h1T, w["wu"], w["wvT"], n_tab, c_tab, r2, e2, h1, row(ln2_g[0]), row(ln2_b[0]))
    return out.reshape(B, S, D)
```

```python
import functools
import math

import jax
import jax.numpy as jnp
from jax import lax
from jax.experimental import pallas as pl
from jax.experimental.pallas import tpu as pltpu

F32 = jnp.float32
BF16 = jnp.bfloat16

D_MODEL = 1024
N_HEADS = 8
QK_NOPE_DIM = 64
QK_ROPE_DIM = 32
ROPE_HALF = QK_ROPE_DIM // 2
QK_DIM = QK_NOPE_DIM + QK_ROPE_DIM
V_DIM = 64
Q_LORA = 256
KV_LORA = 256
ROPE_THETA = 10000.0
F_GROUPS = 4
F_GROUP_DIM = 128
F_DIM = F_GROUPS * F_GROUP_DIM
PEER_HEADS = 8
PEER_NKEYS = 128
PEER_EXPERTS = PEER_NKEYS * PEER_NKEYS
PEER_HALF = 128
PEER_TOPK = 16
DEPTH = 1
DN_ALPHA = (2.0 * DEPTH) ** 0.25
LN_EPS = 1e-5
RMS_EPS = 1e-6

LANES = 128
SUBLANES = 8
HEAD_PAD = LANES
VMEM_LIMIT_BYTES = 56 * 1024 * 1024

TM_FRONT = 256
TQ_ATTN = 2048
TQ_SUB = 256
TK_ATTN = 512
FFT_A_LANES = 4096
FFT_B_K1 = 8
TM_POST = 256
TM_TOPK = 256
TM_PEER = 512
EB_PEER = 2048
PEER_CHAIN = 512

_CAND_GROUPS = ((0, 0, 8), (0, 8, 8), (1, 0, 8), (2, 0, 5), (3, 0, 4), (4, 0, 3), (5, 0, 2), (6, 0, 2), (7, 0, 2))


def _params(sem):
    return pltpu.CompilerParams(dimension_semantics=sem, vmem_limit_bytes=VMEM_LIMIT_BYTES)


def _const_spec(shape):
    nd = len(shape)
    return pl.BlockSpec(shape, lambda *_: (0,) * nd)


def _layer_norm(x, g, b):
    mu = jnp.mean(x, axis=-1, keepdims=True)
    xc = x - mu
    var = jnp.mean(xc * xc, axis=-1, keepdims=True)
    return xc * lax.rsqrt(var + LN_EPS) * g + b


def _rms_norm(x, g):
    return x * lax.rsqrt(jnp.mean(x * x, axis=-1, keepdims=True) + RMS_EPS) * g


def _dot(a, b):
    return jnp.dot(a, b, preferred_element_type=F32)


def _front_kernel(x_ref, g0_ref, b0_ref, wq_ref, bq_ref, gq_ref, wqu_ref, wqr_ref, wkv_ref, bkv_ref, gkv_ref, wku_ref,
                  wvu_ref, wkr_ref, bkr_ref, wkrr_ref, bkrr_ref, cq_ref, sq_ref, ck_ref, sk_ref,
                  wz_ref, bz_ref, cs_ref, wg_ref, bg_ref,
                  h0_ref, qT_ref, k_ref, vT_ref, ur_ref, ui_ref, g_ref):
    h0 = _layer_norm(x_ref[...], g0_ref[...], b0_ref[...])
    h0_ref[...] = h0
    hb = h0.astype(BF16)
    cq = _rms_norm(_dot(hb, wq_ref[...]) + bq_ref[...], gq_ref[...]).astype(BF16)
    ckv = _rms_norm(_dot(hb, wkv_ref[...]) + bkv_ref[...], gkv_ref[...]).astype(BF16)
    cos_q, sin_q = cq_ref[...], sq_ref[...]
    k_rope = ((_dot(hb, wkr_ref[...]) + bkr_ref[...]) * ck_ref[...]
              + (_dot(hb, wkrr_ref[...]) + bkrr_ref[...]) * sk_ref[...])
    for h in range(N_HEADS):
        q_h = _dot(cq, wqu_ref[h]) * cos_q + _dot(cq, wqr_ref[h]) * sin_q
        qT_ref[0, h * HEAD_PAD:(h + 1) * HEAD_PAD, :] = q_h.T.astype(BF16)
        k_ref[0, h] = (_dot(ckv, wku_ref[h]) + k_rope).astype(BF16)
    v_all = _dot(ckv, wvu_ref[...])
    vT_ref[0] = v_all.T.astype(BF16)
    z = (_dot(hb, wz_ref[...]) + bz_ref[...]).astype(BF16)
    cs = cs_ref[...]
    for gi in range(F_GROUPS):
        lo, hi = gi * F_GROUP_DIM, (gi + 1) * F_GROUP_DIM
        pq = _dot(z[:, lo:hi], cs)
        ur_ref[:, lo:hi] = pq[:, :F_GROUP_DIM].astype(BF16)
        ui_ref[:, lo:hi] = (-pq[:, F_GROUP_DIM:]).astype(BF16)
    g_ref[...] = jax.nn.sigmoid(_dot(hb, wg_ref[...]) + bg_ref[...]).astype(BF16)


def _front(x2, B, S, g0, b0, w):
    T = B * S
    tm = TM_FRONT
    nb = S // tm
    tok = lambda b, i: (b * nb + i, 0)
    pos = lambda b, i: (i, 0)
    in_specs = [
        pl.BlockSpec((tm, D_MODEL), tok), _const_spec((1, D_MODEL)), _const_spec((1, D_MODEL)),
        _const_spec((D_MODEL, Q_LORA)), _const_spec((1, Q_LORA)), _const_spec((1, Q_LORA)),
        _const_spec((N_HEADS, Q_LORA, HEAD_PAD)), _const_spec((N_HEADS, Q_LORA, HEAD_PAD)),
        _const_spec((D_MODEL, KV_LORA)), _const_spec((1, KV_LORA)), _const_spec((1, KV_LORA)),
        _const_spec((N_HEADS, KV_LORA, HEAD_PAD)), _const_spec((KV_LORA, N_HEADS * V_DIM)),
        _const_spec((D_MODEL, HEAD_PAD)), _const_spec((1, HEAD_PAD)),
        _const_spec((D_MODEL, HEAD_PAD)), _const_spec((1, HEAD_PAD)),
        pl.BlockSpec((tm, HEAD_PAD), pos), pl.BlockSpec((tm, HEAD_PAD), pos),
        pl.BlockSpec((tm, HEAD_PAD), pos), pl.BlockSpec((tm, HEAD_PAD), pos),
        _const_spec((D_MODEL, F_DIM)), _const_spec((1, F_DIM)), _const_spec((F_GROUP_DIM, 2 * F_GROUP_DIM)),
        _const_spec((D_MODEL, 2 * D_MODEL)), _const_spec((1, 2 * D_MODEL)),
    ]
    out_specs = [
        pl.BlockSpec((tm, D_MODEL), tok),
        pl.BlockSpec((1, N_HEADS * HEAD_PAD, tm), lambda b, i: (b, 0, i)),
        pl.BlockSpec((1, N_HEADS, tm, HEAD_PAD), lambda b, i: (b, 0, i, 0)),
        pl.BlockSpec((1, N_HEADS * V_DIM, tm), lambda b, i: (b, 0, i)),
        pl.BlockSpec((tm, F_DIM), tok), pl.BlockSpec((tm, F_DIM), tok), pl.BlockSpec((tm, 2 * D_MODEL), tok),
    ]
    out_shape = [
        jax.ShapeDtypeStruct((T, D_MODEL), F32),
        jax.ShapeDtypeStruct((B, N_HEADS * HEAD_PAD, S), BF16),
        jax.ShapeDtypeStruct((B, N_HEADS, S, HEAD_PAD), BF16),
        jax.ShapeDtypeStruct((B, N_HEADS * V_DIM, S), BF16),
        jax.ShapeDtypeStruct((T, F_DIM), BF16), jax.ShapeDtypeStruct((T, F_DIM), BF16),
        jax.ShapeDtypeStruct((T, 2 * D_MODEL), BF16),
    ]
    return pl.pallas_call(
        _front_kernel, grid=(B, nb), in_specs=in_specs, out_specs=out_specs, out_shape=out_shape,
        compiler_params=_params(("parallel", "parallel")), name="front",
    )(x2, g0, b0, w["wq"], w["bq"], w["gq"], w["wqu"], w["wqr"], w["wkv"], w["bkv"], w["gkv"], w["wku"], w["wvu"],
      w["wkr"], w["bkr"], w["wkrr"], w["bkrr"], w["cos_q"], w["sin_q"], w["cos_k"], w["sin_k"],
      w["wz"], w["bz"], w["cs128"], w["wg"], w["bg"])


def _col_reduce(x, op, final):
    parts = [x[i:i + SUBLANES] for i in range(0, x.shape[0], SUBLANES)]
    while len(parts) > 1:
        parts = [op(parts[i], parts[i + 1]) for i in range(0, len(parts), 2)]
    return final(parts[0], axis=0, keepdims=True)


def _attn_kernel(qT_ref, k_ref, vT_ref, oT_ref, s_ref, *, tk, n_sub):
    tq = qT_ref.shape[2]
    tsub = tq // n_sub
    n_chunks = k_ref.shape[2] // tk
    subs = range(n_sub)

    def scores(c, slot):
        off = pl.multiple_of(c * tk, tk)
        k_c = k_ref[0, 0, pl.ds(off, tk), :]
        cmax = []
        for u in subs:
            s = _dot(k_c, qT_ref[0, :, u * tsub:(u + 1) * tsub])
            s_ref[slot, u] = s
            cmax.append(_col_reduce(s, jnp.maximum, jnp.max))
        return tuple(cmax)

    def softmax_pv(c, slot, carry, cmax):
        off = pl.multiple_of(c * tk, tk)
        vT_c = vT_ref[0, :, pl.ds(off, tk)]
        m_new = [jnp.maximum(carry[u][0], cmax[u]) for u in subs]
        p = [jnp.exp2(s_ref[slot, u] - m_new[u]) for u in subs]
        alpha = [jnp.exp2(carry[u][0] - m_new[u]) for u in subs]
        l = [alpha[u] * carry[u][1] + _col_reduce(p[u], jnp.add, jnp.sum) for u in subs]
        acc = [alpha[u] * carry[u][2] + _dot(vT_c, p[u].astype(BF16)) for u in subs]
        return tuple((m_new[u], l[u], acc[u]) for u in subs)

    def body(i, carry):
        state, cmax0 = carry
        c0 = 2 * i
        cmax1 = scores(c0 + 1, 1)
        state = softmax_pv(c0, 0, state, cmax0)
        cmax0 = scores(c0 + 2, 0)
        return softmax_pv(c0 + 1, 1, state, cmax1), cmax0

    cmax_first = scores(0, 0)
    init = tuple((jnp.full((1, tsub), -jnp.inf, F32), jnp.zeros((1, tsub), F32), jnp.zeros((V_DIM, tsub), F32))
                 for _ in range(n_sub))
    state, cmax0 = lax.fori_loop(0, n_chunks // 2 - 1, body, (init, cmax_first))
    cmax1 = scores(n_chunks - 1, 1)
    state = softmax_pv(n_chunks - 2, 0, state, cmax0)
    fin = softmax_pv(n_chunks - 1, 1, state, cmax1)
    for u in range(n_sub):
        _, l, acc = fin[u]
        oT_ref[0, :, u * tsub:(u + 1) * tsub] = acc * (1.0 / l)


def _attention(qT, k, vT):
    B, _, S = qT.shape
    tq = TQ_ATTN
    return pl.pallas_call(
        functools.partial(_attn_kernel, tk=min(TK_ATTN, S), n_sub=TQ_ATTN // TQ_SUB),
        grid=(B, N_HEADS, S // tq),
        in_specs=[
            pl.BlockSpec((1, HEAD_PAD, tq), lambda b, h, i: (b, h, i)),
            pl.BlockSpec((1, 1, S, HEAD_PAD), lambda b, h, i: (b, h, 0, 0)),
            pl.BlockSpec((1, V_DIM, S), lambda b, h, i: (b, h, 0)),
        ],
        out_specs=pl.BlockSpec((1, V_DIM, tq), lambda b, h, i: (b, h, i)),
        out_shape=jax.ShapeDtypeStruct((B, N_HEADS * V_DIM, S), F32),
        scratch_shapes=[pltpu.VMEM((2, TQ_ATTN // TQ_SUB, min(TK_ATTN, S), TQ_SUB), F32)],
        compiler_params=_params(("parallel", "parallel", "parallel")),
        name="attention",
    )(qT, k, vT)


def _fft_a_kernel(ur_ref, ui_ref, m1_ref, y_ref):
    u = jnp.concatenate([ur_ref[0], ui_ref[0]], axis=0)
    y_ref[0] = _dot(m1_ref[...], u).astype(BF16)


def _fft_a(ur3, ui3, m1):
    B, n1, W = ur3.shape
    L = min(FFT_A_LANES, W)
    blk = lambda b, j: (b, 0, j)
    return pl.pallas_call(
        _fft_a_kernel, grid=(B, W // L),
        in_specs=[pl.BlockSpec((1, n1, L), blk), pl.BlockSpec((1, n1, L), blk), _const_spec((2 * n1, 2 * n1))],
        out_specs=pl.BlockSpec((1, 2 * n1, L), blk),
        out_shape=jax.ShapeDtypeStruct((B, 2 * n1, W), BF16),
        compiler_params=_params(("parallel", "parallel")), name="fft_a",
    )(ur3, ui3, m1)


def _fft_b_kernel(y_ref, g_ref, o_ref, *, nk, scale):
    for j in range(nk):
        ycat = jnp.concatenate([y_ref[0, 0, j], y_ref[0, 1, j]], axis=0)
        o_ref[0, :, j * F_DIM:(j + 1) * F_DIM] = (_dot(g_ref[j], ycat) * scale).astype(BF16)


def _fft_b(y5, gtab, scale):
    B, _, n1, _, _ = y5.shape
    nk = min(FFT_B_K1, n1)
    return pl.pallas_call(
        functools.partial(_fft_b_kernel, nk=nk, scale=scale), grid=(B, n1 // nk),
        in_specs=[pl.BlockSpec((1, 2, nk, LANES, F_DIM), lambda b, j: (b, 0, j, 0, 0)),
                  pl.BlockSpec((nk, LANES, 2 * LANES), lambda b, j: (j, 0, 0))],
        out_specs=pl.BlockSpec((1, LANES, nk * F_DIM), lambda b, j: (b, 0, j)),
        out_shape=jax.ShapeDtypeStruct((B, LANES, n1 * F_DIM), BF16),
        compiler_params=_params(("parallel", "parallel")), name="fft_b",
    )(y5, gtab)


def _post_kernel(oT_ref, yf_ref, g_ref, h0_ref, wo_ref, wf_ref, wout_ref, bout_ref, lg_ref, lb_ref, h1_ref, h1T_ref):
    o = oT_ref[0].T.astype(BF16)
    y_a = _dot(o, wo_ref[...])
    y_f = _dot(yf_ref[...], wf_ref[...])
    g = g_ref[...].astype(F32)
    m = (g[:, :D_MODEL] * y_a + g[:, D_MODEL:] * y_f).astype(BF16)
    mix = _dot(m, wout_ref[...]) + bout_ref[...]
    h1 = _layer_norm(DN_ALPHA * h0_ref[...] + mix, lg_ref[...], lb_ref[...])
    h1_ref[...] = h1
    h1T_ref[...] = h1.T.astype(BF16)


def _post(oT, yf, g, h0, B, S, w):
    T = B * S
    tm = TM_POST
    nb = S // tm
    tok = lambda b, i: (b * nb + i, 0)
    return pl.pallas_call(
        _post_kernel, grid=(B, nb),
        in_specs=[pl.BlockSpec((1, N_HEADS * V_DIM, tm), lambda b, i: (b, 0, i)),
                  pl.BlockSpec((tm, F_DIM), tok), pl.BlockSpec((tm, 2 * D_MODEL), tok), pl.BlockSpec((tm, D_MODEL), tok),
                  _const_spec((N_HEADS * V_DIM, D_MODEL)), _const_spec((F_DIM, D_MODEL)), _const_spec((D_MODEL, D_MODEL)),
                  _const_spec((1, D_MODEL)), _const_spec((1, D_MODEL)), _const_spec((1, D_MODEL))],
        out_specs=[pl.BlockSpec((tm, D_MODEL), tok), pl.BlockSpec((D_MODEL, tm), lambda b, i: (0, b * nb + i))],
        out_shape=[jax.ShapeDtypeStruct((T, D_MODEL), F32), jax.ShapeDtypeStruct((D_MODEL, T), BF16)],
        compiler_params=_params(("parallel", "parallel")), name="post",
    )(oT, yf, g, h0, w["wo"], w["wf"], w["wout"], w["bout"], w["ln1_g"], w["ln1_b"])


def _extract_top16(s):
    row = lax.broadcasted_iota(jnp.int32, s.shape, 0).astype(F32)
    slot = lax.broadcasted_iota(jnp.int32, (PEER_TOPK, s.shape[1]), 0)
    rank = jnp.full(s.shape, float(PEER_TOPK), F32)
    vals = jnp.zeros((PEER_TOPK, s.shape[1]), F32)
    for r in range(PEER_TOPK):
        m = jnp.max(s, axis=0, keepdims=True)
        idx = jnp.min(jnp.where(s == m, row, float(PEER_NKEYS)), axis=0, keepdims=True)
        hit = row == idx
        s = jnp.where(hit, -jnp.inf, s)
        rank = jnp.where(hit, float(r), rank)
        vals = jnp.where(slot == r, m, vals)
    return vals, rank


def _cand_rows():
    groups = [[(i, j0 + r) if r < valid else None for r in range(SUBLANES)] for (i, j0, valid) in _CAND_GROUPS]
    groups.append([(SUBLANES + r, 0) for r in range(SUBLANES)])
    return groups


def _static_beats(cp, c):
    if cp == c:
        return 0
    if cp[0] <= c[0] and cp[1] <= c[1]:
        return 1
    if cp[0] >= c[0] and cp[1] >= c[1]:
        return 0
    return None


def _static_counts():
    rows = _cand_rows()
    out = [[0.0] * SUBLANES for _ in rows]
    for cg in rows:
        for cp in cg:
            if cp is None:
                continue
            for g, tgt in enumerate(rows):
                res = [_static_beats(cp, c) if c is not None else 0 for c in tgt]
                if all(v is not None for v in res):
                    for r in range(SUBLANES):
                        out[g][r] += float(res[r])
    return out


def _select_pairs(a_rep, b_lo, b_hi, a_hi, static_counts):
    L = b_lo.shape[1]
    sub = lax.broadcasted_iota(jnp.int32, (SUBLANES, L), 0)
    rows = _cand_rows()
    groups = []
    for (i, j0, valid) in _CAND_GROUPS:
        v = a_rep[i] + (b_lo if j0 == 0 else b_hi)
        if valid < SUBLANES:
            v = jnp.where(sub < valid, v, -jnp.inf)
        groups.append(v)
    groups.append(a_hi + jnp.broadcast_to(b_lo[0:1, :], (SUBLANES, L)))
    n_groups = len(groups)
    counts = [static_counts[g] for g in range(n_groups)]
    for gp in range(n_groups):
        for rp in range(SUBLANES):
            cp = rows[gp][rp]
            if cp is None:
                continue
            vb = jnp.broadcast_to(groups[gp][rp:rp + 1, :], (SUBLANES, L))
            for g in range(n_groups):
                if all(c is None or _static_beats(cp, c) is not None for c in rows[g]):
                    continue
                if g < gp:
                    beats = jnp.where(vb > groups[g], 1.0, 0.0)
                elif g > gp:
                    beats = jnp.where(vb >= groups[g], 1.0, 0.0)
                else:
                    beats = jnp.where(sub > rp, jnp.where(vb >= groups[g], 1.0, 0.0), jnp.where(vb > groups[g], 1.0, 0.0))
                counts[g] = counts[g] + beats
    top = groups[0][0:1, :]
    valid_rows = [g[2] for g in _CAND_GROUPS] + [SUBLANES]
    sel = []
    z = jnp.zeros((1, L), F32)
    for g in range(n_groups):
        s_g = jnp.where(counts[g] < float(PEER_TOPK), 1.0, 0.0)
        if valid_rows[g] < SUBLANES:
            s_g = jnp.where(sub < valid_rows[g], s_g, 0.0)
        sel.append(s_g)
        z = z + jnp.sum(s_g * jnp.exp(groups[g] - top), axis=0, keepdims=True)
    n = [jnp.sum(sel[0] + sel[1], axis=0, keepdims=True)]
    for g in range(2, n_groups - 1):
        n.append(jnp.sum(sel[g], axis=0, keepdims=True))
    for r in range(SUBLANES):
        n.append(sel[n_groups - 1][r:r + 1, :])
    return n, z


def _rows_from_rep(rep, lo):
    sub = lax.broadcasted_iota(jnp.int32, rep[0].shape, 0)
    out = rep[lo]
    for r in range(1, SUBLANES):
        out = jnp.where(sub == r, rep[lo + r], out)
    return out


def _oddeven_merge_pairs(n):
    pairs = []
    t = n.bit_length() - 1
    for pi in range(t):
        p = 1 << pi
        for ki in range(pi, -1, -1):
            k = 1 << ki
            for j in range(k % p, n - k, 2 * k):
                for i in range(min(k, n - j - k)):
                    if (i + j) // (2 * p) == (i + j + k) // (2 * p):
                        pairs.append((i + j, i + j + k))
    return pairs


_SORT16 = _oddeven_merge_pairs(PEER_TOPK)


def _top16_values(s):
    w = [s[g * SUBLANES:(g + 1) * SUBLANES] for g in range(PEER_NKEYS // SUBLANES)]
    for (i, j) in _SORT16:
        w[i], w[j] = jnp.maximum(w[i], w[j]), jnp.minimum(w[i], w[j])
    for shift in (4, 2, 1):
        other = [pltpu.roll(x, shift, axis=0) for x in w]
        w = [jnp.maximum(w[i], other[PEER_TOPK - 1 - i]) for i in range(PEER_TOPK)]
        d = PEER_TOPK // 2
        while d >= 1:
            for i in range(PEER_TOPK):
                if i & d == 0:
                    w[i], w[i + d] = jnp.maximum(w[i], w[i + d]), jnp.minimum(w[i], w[i + d])
            d //= 2
    return w


def _tie_flags(s, w):
    flag = jnp.zeros_like(w[0])
    for r in range(PEER_TOPK - 1):
        flag = flag + jnp.where(w[r] == w[r + 1], 1.0, 0.0)
    parts = [jnp.where(s[g * SUBLANES:(g + 1) * SUBLANES] >= w[PEER_TOPK - 1], 1.0, 0.0)
             for g in range(PEER_NKEYS // SUBLANES)]
    while len(parts) > 1:
        parts = [parts[i] + parts[i + 1] for i in range(0, len(parts), 2)]
    count = jnp.sum(parts[0], axis=0, keepdims=True)
    return flag + (count - float(PEER_TOPK))


def _peer_topk_kernel(h1T_ref, wqT_ref, keys_ref, sc_ref, n_ref, c_ref, r2_ref, e2_ref, qp_ref):
    tm = h1T_ref.shape[1]
    qp_ref[...] = _dot(wqT_ref[...], h1T_ref[...]).astype(BF16)

    def scores(h, lo):
        r0, r1 = 2 * h * PEER_HALF, (2 * h + 1) * PEER_HALF
        if not isinstance(h, int):
            r0, r1 = pl.multiple_of(r0, PEER_HALF), pl.multiple_of(r1, PEER_HALF)
        s1 = _dot(keys_ref[2 * h], qp_ref[pl.ds(r0, PEER_HALF), pl.ds(lo, LANES)])
        s2 = _dot(keys_ref[2 * h + 1], qp_ref[pl.ds(r1, PEER_HALF), pl.ds(lo, LANES)])
        return s1, s2

    def store(h, lo, n_a, c_a, rank2, e2):
        n_ref[h, :, pl.ds(lo, LANES)] = n_a
        c_ref[h, :, pl.ds(lo, LANES)] = c_a
        r2_ref[h, :, pl.ds(lo, LANES)] = rank2.astype(BF16)
        e2_ref[h, :, pl.ds(lo, LANES)] = e2.astype(BF16)

    def fast_head(h, lo):
        s1, s2 = scores(h, lo)
        w1, w2 = _top16_values(s1), _top16_values(s2)
        n, z = _select_pairs(w1, _rows_from_rep(w2, 0), _rows_from_rep(w2, SUBLANES), _rows_from_rep(w1, SUBLANES),
                             sc_ref)
        n_parts, r_parts = [], []
        for g in range(PEER_NKEYS // SUBLANES):
            s1_g, s2_g = s1[g * SUBLANES:(g + 1) * SUBLANES], s2[g * SUBLANES:(g + 1) * SUBLANES]
            n_g = jnp.zeros_like(s1_g)
            r_g = jnp.full_like(s2_g, float(PEER_TOPK))
            for r in range(PEER_TOPK):
                n_g = jnp.where(s1_g == w1[r], n[r], n_g)
                r_g = jnp.where(s2_g == w2[r], float(r), r_g)
            n_parts.append(n_g)
            r_parts.append(r_g)
        c_a = jnp.exp(s1 - w1[0][0:1, :]) * (1.0 / z)
        e2 = jnp.exp(s2 - w2[0][0:1, :])
        store(h, lo, jnp.concatenate(n_parts, axis=0), c_a, jnp.concatenate(r_parts, axis=0), e2)
        return _tie_flags(s1, w1) + _tie_flags(s2, w2)

    def exact_head(h, lo):
        s1, s2 = scores(h, lo)
        a_vals, rank1 = _extract_top16(s1)
        b_vals, rank2 = _extract_top16(s2)
        a_rep = [jnp.broadcast_to(a_vals[i:i + 1, :], (SUBLANES, LANES)) for i in range(PEER_TOPK)]
        n, z = _select_pairs(a_rep, b_vals[:SUBLANES], b_vals[SUBLANES:], a_vals[SUBLANES:], sc_ref)
        n_a = jnp.zeros_like(s1)
        for i in range(PEER_TOPK):
            n_a = n_a + jnp.where(rank1 == float(i), n[i], 0.0)
        c_a = jnp.exp(s1 - a_vals[0:1, :]) * (1.0 / z)
        e2 = jnp.exp(s2 - b_vals[0:1, :])
        store(h, lo, n_a, c_a, rank2, e2)

    def chunk(ci, _):
        lo = pl.multiple_of(ci * LANES, LANES)
        flags = jnp.zeros((SUBLANES, LANES), F32)
        for h in range(PEER_HEADS):
            flags = flags + fast_head(h, lo)

        @pl.when(jnp.max(flags) > 0.0)
        def _():
            def per_head(h, carry):
                exact_head(h, lo)
                return carry
            lax.fori_loop(0, PEER_HEADS, per_head, 0)
        return 0

    lax.fori_loop(0, tm // LANES, chunk, 0)


def _peer_topk(h1T, wqT, keys):
    T = h1T.shape[1]
    tm = TM_TOPK
    nq = wqT.shape[0]
    col = lambda i: (0, i)
    col3 = lambda i: (0, 0, i)
    static = jnp.broadcast_to(jnp.asarray(_static_counts(), F32)[:, :, None], (len(_CAND_GROUPS) + 1, SUBLANES, LANES))
    return pl.pallas_call(
        _peer_topk_kernel, grid=(T // tm,),
        in_specs=[pl.BlockSpec((D_MODEL, tm), col), _const_spec((nq, D_MODEL)),
                  _const_spec((2 * PEER_HEADS, PEER_NKEYS, PEER_HALF)), _const_spec(static.shape)],
        out_specs=[pl.BlockSpec((PEER_HEADS, PEER_NKEYS, tm), col3), pl.BlockSpec((PEER_HEADS, PEER_NKEYS, tm), col3),
                   pl.BlockSpec((PEER_HEADS, PEER_NKEYS, tm), col3), pl.BlockSpec((PEER_HEADS, PEER_NKEYS, tm), col3)],
        out_shape=[jax.ShapeDtypeStruct((PEER_HEADS, PEER_NKEYS, T), F32),
                   jax.ShapeDtypeStruct((PEER_HEADS, PEER_NKEYS, T), F32),
                   jax.ShapeDtypeStruct((PEER_HEADS, PEER_NKEYS, T), BF16),
                   jax.ShapeDtypeStruct((PEER_HEADS, PEER_NKEYS, T), BF16)],
        scratch_shapes=[pltpu.VMEM((nq, tm), BF16)],
        compiler_params=_params(("parallel",)), name="peer_topk",
    )(h1T, wqT, keys, static)


def _peer_dense_kernel(xT_ref, wu_ref, wvT_ref, n_ref, c_ref, r2_ref, e2_ref, h1_ref, lg_ref, lb_ref, out_ref,
                       acc_ref, *, eb):
    j = pl.program_id(1)
    tm = xT_ref.shape[1]

    @pl.when(j == 0)
    def _():
        acc_ref[...] = jnp.zeros_like(acc_ref)

    n_chains = eb // PEER_CHAIN
    rows = [slice(ch * PEER_CHAIN, (ch + 1) * PEER_CHAIN) for ch in range(n_chains)]

    def first_matmul(ch):
        return _dot(wu_ref[rows[ch], :], xT_ref[...])

    def gate_gelu(ch, hpre):
        acts = []
        for aa in range(PEER_CHAIN // PEER_NKEYS):
            a_key = (j * n_chains + ch) * (PEER_CHAIN // PEER_NKEYS) + aa
            gate = jnp.zeros((PEER_NKEYS, tm), BF16)
            for h in range(PEER_HEADS):
                n_b = jnp.broadcast_to(n_ref[h, pl.ds(a_key, 1), :], (PEER_NKEYS, tm)).astype(BF16)
                c_b = jnp.broadcast_to(c_ref[h, pl.ds(a_key, 1), :], (PEER_NKEYS, tm)).astype(BF16)
                gate = gate + jnp.where(r2_ref[h] < n_b, e2_ref[h], jnp.zeros((), BF16)) * c_b
            hp = hpre[aa * PEER_NKEYS:(aa + 1) * PEER_NKEYS, :]
            act = 0.5 * hp * (1.0 + lax.erf(hp * (1.0 / math.sqrt(2.0))))
            acts.append(act.astype(BF16) * gate)
        return jnp.concatenate(acts, axis=0)

    hpre, act = {}, {}
    for t in range(n_chains + 2):
        if t < n_chains:
            hpre[t] = first_matmul(t)
        if 0 <= t - 1 < n_chains:
            act[t - 1] = gate_gelu(t - 1, hpre.pop(t - 1))
        if 0 <= t - 2 < n_chains:
            acc_ref[...] += _dot(wvT_ref[:, rows[t - 2]], act.pop(t - 2))

    @pl.when(j == pl.num_programs(1) - 1)
    def _():
        y = acc_ref[...].T
        out_ref[...] = _layer_norm(DN_ALPHA * h1_ref[...] + y, lg_ref[...], lb_ref[...])


def _peer_dense(h1T, wu, wvT, n_tab, c_tab, r2, e2, h1, lg, lb):
    T = h1T.shape[1]
    tm, eb = TM_PEER, EB_PEER
    col = lambda i, j: (0, i)
    col3 = lambda i, j: (0, 0, i)
    row = lambda i, j: (i, 0)
    return pl.pallas_call(
        functools.partial(_peer_dense_kernel, eb=eb), grid=(T // tm, PEER_EXPERTS // eb),
        in_specs=[pl.BlockSpec((D_MODEL, tm), col),
                  pl.BlockSpec((eb, D_MODEL), lambda i, j: (j, 0)),
                  pl.BlockSpec((D_MODEL, eb), lambda i, j: (0, j)),
                  pl.BlockSpec((PEER_HEADS, PEER_NKEYS, tm), col3), pl.BlockSpec((PEER_HEADS, PEER_NKEYS, tm), col3),
                  pl.BlockSpec((PEER_HEADS, PEER_NKEYS, tm), col3), pl.BlockSpec((PEER_HEADS, PEER_NKEYS, tm), col3),
                  pl.BlockSpec((tm, D_MODEL), row), _const_spec((1, D_MODEL)), _const_spec((1, D_MODEL))],
        out_specs=pl.BlockSpec((tm, D_MODEL), row),
        out_shape=jax.ShapeDtypeStruct((T, D_MODEL), F32),
        scratch_shapes=[pltpu.VMEM((D_MODEL, tm), F32)],
        compiler_params=_params(("parallel", "arbitrary")), name="peer_dense",
    )(h1T, wu, wvT, n_tab, c_tab, r2, e2, h1, lg, lb)


def _pad_cols(w, lo, total):
    return jnp.pad(w, ((0, 0), (lo, total - lo - w.shape[1])))


def _prepare(S, w_in, b_in, q_norm_g, kv_norm_g, w_uq, w_uk, w_uv, w_o_attn, w_fourier, w_out, b_out, ln1_g, ln1_b,
             peer_w_q, peer_keys, peer_w_u, peer_w_v):
    o_kv, o_kr, o_z, o_g = Q_LORA, Q_LORA + KV_LORA, Q_LORA + KV_LORA + QK_ROPE_DIM, Q_LORA + KV_LORA + QK_ROPE_DIM + F_DIM
    row = lambda v: v.reshape(1, -1).astype(F32)
    w = {}
    w["wq"], w["bq"], w["gq"] = w_in[:, :o_kv].astype(BF16), row(b_in[:o_kv]), row(q_norm_g)
    w["wkv"], w["bkv"], w["gkv"] = w_in[:, o_kv:o_kr].astype(BF16), row(b_in[o_kv:o_kr]), row(kv_norm_g)
    wkr, bkr = w_in[:, o_kr:o_z], b_in[o_kr:o_z].reshape(1, -1)
    swap = lambda m: jnp.concatenate([m[:, ROPE_HALF:], m[:, :ROPE_HALF]], axis=1)
    w["wkr"], w["bkr"] = _pad_cols(wkr, QK_NOPE_DIM, HEAD_PAD).astype(BF16), _pad_cols(bkr, QK_NOPE_DIM, HEAD_PAD).astype(F32)
    w["wkrr"] = _pad_cols(swap(wkr), QK_NOPE_DIM, HEAD_PAD).astype(BF16)
    w["bkrr"] = _pad_cols(swap(bkr), QK_NOPE_DIM, HEAD_PAD).astype(F32)
    w["wz"], w["bz"] = w_in[:, o_z:o_g].astype(BF16), row(b_in[o_z:o_g])
    w["wg"], w["bg"] = w_in[:, o_g:].astype(BF16), row(b_in[o_g:])
    wq3 = w_uq.reshape(Q_LORA, N_HEADS, QK_DIM).transpose(1, 0, 2)
    rope3 = wq3[:, :, QK_NOPE_DIM:]
    rope3_sw = jnp.concatenate([rope3[:, :, ROPE_HALF:], rope3[:, :, :ROPE_HALF]], axis=2)
    padh = lambda m: jnp.pad(m, ((0, 0), (0, 0), (0, HEAD_PAD - m.shape[2])))
    w["wqu"] = padh(wq3).astype(BF16)
    w["wqr"] = padh(jnp.concatenate([jnp.zeros_like(wq3[:, :, :QK_NOPE_DIM]), rope3_sw], axis=2)).astype(BF16)
    wk3 = w_uk.reshape(KV_LORA, N_HEADS, QK_NOPE_DIM).transpose(1, 0, 2)
    w["wku"] = padh(wk3).astype(BF16)
    w["wvu"] = w_uv.astype(BF16)
    pos = jnp.arange(S, dtype=F32)
    inv = 1.0 / (ROPE_THETA ** (jnp.arange(0, QK_ROPE_DIM, 2, dtype=F32) / QK_ROPE_DIM))
    ang = pos[:, None] * inv[None, :]
    cos, sin = jnp.cos(ang), jnp.sin(ang)
    ones, zeros = jnp.ones((S, QK_NOPE_DIM), F32), jnp.zeros((S, QK_NOPE_DIM), F32)
    tail = jnp.zeros((S, HEAD_PAD - QK_DIM), F32)
    scale = QK_DIM ** -0.5 * math.log2(math.e)
    w["cos_q"] = jnp.concatenate([ones, cos, cos, tail], axis=1) * scale
    w["sin_q"] = jnp.concatenate([zeros, -sin, sin, tail], axis=1) * scale
    w["cos_k"] = jnp.concatenate([zeros, cos, cos, tail], axis=1)
    w["sin_k"] = jnp.concatenate([zeros, -sin, sin, tail], axis=1)
    n1 = S // LANES
    idx = lambda n: jnp.arange(n, dtype=jnp.int32)
    ang_c = (2.0 * math.pi / F_GROUP_DIM) * ((idx(F_GROUP_DIM)[:, None] * idx(F_GROUP_DIM)[None, :]) % F_GROUP_DIM).astype(F32)
    w["cs128"] = jnp.concatenate([jnp.cos(ang_c), jnp.sin(ang_c)], axis=1).astype(BF16)
    ang_1 = (2.0 * math.pi / n1) * ((idx(n1)[:, None] * idx(n1)[None, :]) % n1).astype(F32)
    c1, s1 = jnp.cos(ang_1), jnp.sin(ang_1)
    w["m1"] = jnp.concatenate([jnp.concatenate([c1, s1], axis=1), jnp.concatenate([-s1, c1], axis=1)], axis=0).astype(BF16)
    kk = idx(n1)[:, None, None] + n1 * idx(LANES)[None, :, None]
    ang_g = (2.0 * math.pi / S) * ((kk * idx(LANES)[None, None, :]) % S).astype(F32)
    w["gtab"] = jnp.concatenate([jnp.cos(ang_g), jnp.sin(ang_g)], axis=2).astype(BF16)
    w["wo"], w["wf"], w["wout"] = w_o_attn.astype(BF16), w_fourier.astype(BF16), w_out.astype(BF16)
    w["bout"], w["ln1_g"], w["ln1_b"] = row(b_out), row(ln1_g), row(ln1_b)
    w["wqT"] = peer_w_q.T.astype(BF16)
    w["keys"] = peer_keys.reshape(2 * PEER_HEADS, PEER_NKEYS, PEER_HALF).astype(BF16)
    w["wu"] = peer_w_u.astype(BF16)
    w["wvT"] = peer_w_v.T.astype(BF16)
    return w


def kernel(x, ln0_g, ln0_b, w_in, b_in, q_norm_g, kv_norm_g, w_uq, w_uk, w_uv, w_o_attn, w_fourier, w_out, b_out, ln1_g,
           ln1_b, peer_w_q, peer_keys, peer_w_u, peer_w_v, ln2_g, ln2_b):
    B, S, D = x.shape
    assert D == D_MODEL and w_in.shape[0] == DEPTH
    T = B * S
    n1 = S // LANES
    assert S % max(TM_FRONT, TQ_ATTN, TM_POST, LANES * SUBLANES) == 0 and T % max(TM_PEER, TM_TOPK) == 0
    row = lambda v: v.reshape(1, -1).astype(F32)
    h = x.reshape(T, D)
    w = _prepare(S, w_in[0], b_in[0], q_norm_g[0], kv_norm_g[0], w_uq[0], w_uk[0], w_uv[0], w_o_attn[0], w_fourier[0],
                 w_out[0], b_out[0], ln1_g[0], ln1_b[0], peer_w_q[0], peer_keys[0], peer_w_u[0], peer_w_v[0])
    h0, qT, k, vT, ur, ui, g = _front(h, B, S, row(ln0_g), row(ln0_b), w)
    oT = _attention(qT, k, vT)
    y = _fft_a(ur.reshape(B, n1, LANES * F_DIM), ui.reshape(B, n1, LANES * F_DIM), w["m1"])
    yf = _fft_b(y.reshape(B, 2, n1, LANES, F_DIM), w["gtab"], 1.0 / math.sqrt(S * F_GROUP_DIM))
    h1, h1T = _post(oT, yf.reshape(T, F_DIM), g, h0, B, S, w)
    n_tab, c_tab, r2, e2 = _peer_topk(h1T, w["wqT"], w["keys"])
    out = _peer_dense(h1T, w["wu"], w["wvT"], n_tab, c_tab, r2, e2, h1, row(ln2_g[0]), row(ln2_b[0]))
    return out.reshape(B, S, D)
```

```python
import functools
import math

import jax
import jax.numpy as jnp
from jax import lax
from jax.experimental import pallas as pl
from jax.experimental.pallas import tpu as pltpu

F32 = jnp.float32
BF16 = jnp.bfloat16

D_MODEL = 1024
N_HEADS = 8
QK_NOPE_DIM = 64
QK_ROPE_DIM = 32
ROPE_HALF = QK_ROPE_DIM // 2
QK_DIM = QK_NOPE_DIM + QK_ROPE_DIM
V_DIM = 64
Q_LORA = 256
KV_LORA = 256
ROPE_THETA = 10000.0
F_GROUPS = 4
F_GROUP_DIM = 128
F_DIM = F_GROUPS * F_GROUP_DIM
PEER_HEADS = 8
PEER_NKEYS = 128
PEER_EXPERTS = PEER_NKEYS * PEER_NKEYS
PEER_HALF = 128
PEER_TOPK = 16
DEPTH = 1
DN_ALPHA = (2.0 * DEPTH) ** 0.25
LN_EPS = 1e-5
RMS_EPS = 1e-6
GELU_HALF = 0.5

LANES = 128
SUBLANES = 8
HEAD_PAD = LANES
VMEM_LIMIT_BYTES = 56 * 1024 * 1024

TM_FRONT = 256
TQ_ATTN = 2048
TQ_SUB = 256
TK_ATTN = 512
FFT_A_LANES = 8192
FFT_B_K1 = 16
TM_POST = 512
TM_TOPK = 256
TM_PEER = 512
EB_PEER = 2048
PEER_CHAIN = 512

_CAND_GROUPS = ((0, 0, 8), (0, 8, 8), (1, 0, 8), (2, 0, 5), (3, 0, 4), (4, 0, 3), (5, 0, 2), (6, 0, 2), (7, 0, 2))


def _params(sem):
    return pltpu.CompilerParams(dimension_semantics=sem, vmem_limit_bytes=VMEM_LIMIT_BYTES)


def _const_spec(shape):
    nd = len(shape)
    return pl.BlockSpec(shape, lambda *_: (0,) * nd)


def _layer_norm(x, g, b):
    mu = jnp.mean(x, axis=-1, keepdims=True)
    xc = x - mu
    var = jnp.mean(xc * xc, axis=-1, keepdims=True)
    return xc * lax.rsqrt(var + LN_EPS) * g + b


def _rms_norm(x, g):
    return x * lax.rsqrt(jnp.mean(x * x, axis=-1, keepdims=True) + RMS_EPS) * g


def _dot(a, b):
    return jnp.dot(a, b, preferred_element_type=F32)


def _front_kernel(x_ref, g0_ref, b0_ref, wq_ref, bq_ref, gq_ref, wqu_ref, wqr_ref, wkv_ref, bkv_ref, gkv_ref, wku_ref,
                  wvu_ref, wkr_ref, bkr_ref, wkrr_ref, bkrr_ref, cq_ref, sq_ref, ck_ref, sk_ref,
                  wz_ref, bz_ref, cs_ref, wg_ref, bg_ref,
                  h0_ref, qT_ref, k_ref, vT_ref, ur_ref, ui_ref, g_ref):
    h0 = _layer_norm(x_ref[...], g0_ref[...], b0_ref[...])
    h0_ref[...] = h0
    hb = h0.astype(BF16)
    cq = _rms_norm(_dot(hb, wq_ref[...]) + bq_ref[...], gq_ref[...]).astype(BF16)
    ckv = _rms_norm(_dot(hb, wkv_ref[...]) + bkv_ref[...], gkv_ref[...]).astype(BF16)
    cos_q, sin_q = cq_ref[...], sq_ref[...]
    k_rope = ((_dot(hb, wkr_ref[...]) + bkr_ref[...]) * ck_ref[...]
              + (_dot(hb, wkrr_ref[...]) + bkrr_ref[...]) * sk_ref[...])
    for h in range(N_HEADS):
        q_h = _dot(cq, wqu_ref[h]) * cos_q + _dot(cq, wqr_ref[h]) * sin_q
        qT_ref[0, h * HEAD_PAD:(h + 1) * HEAD_PAD, :] = q_h.T.astype(BF16)
        k_ref[0, h] = (_dot(ckv, wku_ref[h]) + k_rope).astype(BF16)
    v_all = _dot(ckv, wvu_ref[...])
    vT_ref[0] = v_all.T.astype(BF16)
    z = (_dot(hb, wz_ref[...]) + bz_ref[...]).astype(BF16)
    cs = cs_ref[...]
    for gi in range(F_GROUPS):
        lo, hi = gi * F_GROUP_DIM, (gi + 1) * F_GROUP_DIM
        pq = _dot(z[:, lo:hi], cs)
        ur_ref[:, lo:hi] = pq[:, :F_GROUP_DIM].astype(BF16)
        ui_ref[:, lo:hi] = (-pq[:, F_GROUP_DIM:]).astype(BF16)
    g_ref[...] = jax.nn.sigmoid(_dot(hb, wg_ref[...]) + bg_ref[...]).astype(BF16)


def _front(x2, B, S, g0, b0, w):
    T = B * S
    tm = TM_FRONT
    nb = S // tm
    tok = lambda b, i: (b * nb + i, 0)
    pos = lambda b, i: (i, 0)
    in_specs = [
        pl.BlockSpec((tm, D_MODEL), tok), _const_spec((1, D_MODEL)), _const_spec((1, D_MODEL)),
        _const_spec((D_MODEL, Q_LORA)), _const_spec((1, Q_LORA)), _const_spec((1, Q_LORA)),
        _const_spec((N_HEADS, Q_LORA, HEAD_PAD)), _const_spec((N_HEADS, Q_LORA, HEAD_PAD)),
        _const_spec((D_MODEL, KV_LORA)), _const_spec((1, KV_LORA)), _const_spec((1, KV_LORA)),
        _const_spec((N_HEADS, KV_LORA, HEAD_PAD)), _const_spec((KV_LORA, N_HEADS * V_DIM)),
        _const_spec((D_MODEL, HEAD_PAD)), _const_spec((1, HEAD_PAD)),
        _const_spec((D_MODEL, HEAD_PAD)), _const_spec((1, HEAD_PAD)),
        pl.BlockSpec((tm, HEAD_PAD), pos), pl.BlockSpec((tm, HEAD_PAD), pos),
        pl.BlockSpec((tm, HEAD_PAD), pos), pl.BlockSpec((tm, HEAD_PAD), pos),
        _const_spec((D_MODEL, F_DIM)), _const_spec((1, F_DIM)), _const_spec((F_GROUP_DIM, 2 * F_GROUP_DIM)),
        _const_spec((D_MODEL, 2 * D_MODEL)), _const_spec((1, 2 * D_MODEL)),
    ]
    out_specs = [
        pl.BlockSpec((tm, D_MODEL), tok),
        pl.BlockSpec((1, N_HEADS * HEAD_PAD, tm), lambda b, i: (b, 0, i)),
        pl.BlockSpec((1, N_HEADS, tm, HEAD_PAD), lambda b, i: (b, 0, i, 0)),
        pl.BlockSpec((1, N_HEADS * V_DIM, tm), lambda b, i: (b, 0, i)),
        pl.BlockSpec((tm, F_DIM), tok), pl.BlockSpec((tm, F_DIM), tok), pl.BlockSpec((tm, 2 * D_MODEL), tok),
    ]
    out_shape = [
        jax.ShapeDtypeStruct((T, D_MODEL), F32),
        jax.ShapeDtypeStruct((B, N_HEADS * HEAD_PAD, S), BF16),
        jax.ShapeDtypeStruct((B, N_HEADS, S, HEAD_PAD), BF16),
        jax.ShapeDtypeStruct((B, N_HEADS * V_DIM, S), BF16),
        jax.ShapeDtypeStruct((T, F_DIM), BF16), jax.ShapeDtypeStruct((T, F_DIM), BF16),
        jax.ShapeDtypeStruct((T, 2 * D_MODEL), BF16),
    ]
    return pl.pallas_call(
        _front_kernel, grid=(B, nb), in_specs=in_specs, out_specs=out_specs, out_shape=out_shape,
        compiler_params=_params(("parallel", "parallel")), name="front",
    )(x2, g0, b0, w["wq"], w["bq"], w["gq"], w["wqu"], w["wqr"], w["wkv"], w["bkv"], w["gkv"], w["wku"], w["wvu"],
      w["wkr"], w["bkr"], w["wkrr"], w["bkrr"], w["cos_q"], w["sin_q"], w["cos_k"], w["sin_k"],
      w["wz"], w["bz"], w["cs128"], w["wg"], w["bg"])


def _col_reduce(x, op, final):
    parts = [x[i:i + SUBLANES] for i in range(0, x.shape[0], SUBLANES)]
    while len(parts) > 1:
        parts = [op(parts[i], parts[i + 1]) for i in range(0, len(parts), 2)]
    return final(parts[0], axis=0, keepdims=True)


def _attn_kernel(qT_ref, k_ref, vT_ref, oT_ref, s_ref, *, tk, n_sub):
    tq = qT_ref.shape[2]
    tsub = tq // n_sub
    n_chunks = k_ref.shape[2] // tk
    subs = range(n_sub)

    def scores(c, slot):
        off = pl.multiple_of(c * tk, tk)
        k_c = k_ref[0, 0, pl.ds(off, tk), :]
        cmax = []
        for u in subs:
            s = _dot(k_c, qT_ref[0, :, u * tsub:(u + 1) * tsub])
            s_ref[slot, u] = s
            cmax.append(_col_reduce(s, jnp.maximum, jnp.max))
        return tuple(cmax)

    def softmax_pv(c, slot, carry, cmax):
        off = pl.multiple_of(c * tk, tk)
        vT_c = vT_ref[0, :, pl.ds(off, tk)]
        m_new = [jnp.maximum(carry[u][0], cmax[u]) for u in subs]
        p = [jnp.exp2(s_ref[slot, u] - m_new[u]) for u in subs]
        alpha = [jnp.exp2(carry[u][0] - m_new[u]) for u in subs]
        l = [alpha[u] * carry[u][1] + _col_reduce(p[u], jnp.add, jnp.sum) for u in subs]
        acc = [alpha[u] * carry[u][2] + _dot(vT_c, p[u].astype(BF16)) for u in subs]
        return tuple((m_new[u], l[u], acc[u]) for u in subs)

    def body(i, carry):
        state, cmax0 = carry
        c0 = 2 * i
        cmax1 = scores(c0 + 1, 1)
        state = softmax_pv(c0, 0, state, cmax0)
        cmax0 = scores(c0 + 2, 0)
        return softmax_pv(c0 + 1, 1, state, cmax1), cmax0

    cmax_first = scores(0, 0)
    init = tuple((jnp.full((1, tsub), -jnp.inf, F32), jnp.zeros((1, tsub), F32), jnp.zeros((V_DIM, tsub), F32))
                 for _ in range(n_sub))
    state, cmax0 = lax.fori_loop(0, n_chunks // 2 - 1, body, (init, cmax_first))
    cmax1 = scores(n_chunks - 1, 1)
    state = softmax_pv(n_chunks - 2, 0, state, cmax0)
    fin = softmax_pv(n_chunks - 1, 1, state, cmax1)
    for u in range(n_sub):
        _, l, acc = fin[u]
        oT_ref[0, :, u * tsub:(u + 1) * tsub] = acc * (1.0 / l)


def _attention(qT, k, vT):
    B, _, S = qT.shape
    tq = TQ_ATTN
    return pl.pallas_call(
        functools.partial(_attn_kernel, tk=min(TK_ATTN, S), n_sub=TQ_ATTN // TQ_SUB),
        grid=(B, N_HEADS, S // tq),
        in_specs=[
            pl.BlockSpec((1, HEAD_PAD, tq), lambda b, h, i: (b, h, i)),
            pl.BlockSpec((1, 1, S, HEAD_PAD), lambda b, h, i: (b, h, 0, 0)),
            pl.BlockSpec((1, V_DIM, S), lambda b, h, i: (b, h, 0)),
        ],
        out_specs=pl.BlockSpec((1, V_DIM, tq), lambda b, h, i: (b, h, i)),
        out_shape=jax.ShapeDtypeStruct((B, N_HEADS * V_DIM, S), F32),
        scratch_shapes=[pltpu.VMEM((2, TQ_ATTN // TQ_SUB, min(TK_ATTN, S), TQ_SUB), F32)],
        compiler_params=_params(("parallel", "parallel", "parallel")),
        name="attention",
    )(qT, k, vT)


def _fft_a_kernel(ur_ref, ui_ref, m1_ref, y_ref):
    u = jnp.concatenate([ur_ref[0], ui_ref[0]], axis=0)
    y_ref[0] = _dot(m1_ref[...], u).astype(BF16)


def _fft_a(ur3, ui3, m1):
    B, n1, W = ur3.shape
    L = min(FFT_A_LANES, W)
    blk = lambda b, j: (b, 0, j)
    return pl.pallas_call(
        _fft_a_kernel, grid=(B, W // L),
        in_specs=[pl.BlockSpec((1, n1, L), blk), pl.BlockSpec((1, n1, L), blk), _const_spec((2 * n1, 2 * n1))],
        out_specs=pl.BlockSpec((1, 2 * n1, L), blk),
        out_shape=jax.ShapeDtypeStruct((B, 2 * n1, W), BF16),
        compiler_params=_params(("parallel", "parallel")), name="fft_a",
    )(ur3, ui3, m1)


def _fft_b_kernel(y_ref, g_ref, o_ref, *, nk, scale):
    for j in range(nk):
        ycat = jnp.concatenate([y_ref[0, 0, j], y_ref[0, 1, j]], axis=0)
        o_ref[0, :, j * F_DIM:(j + 1) * F_DIM] = (_dot(g_ref[j], ycat) * scale).astype(BF16)


def _fft_b(y5, gtab, scale):
    B, _, n1, _, _ = y5.shape
    nk = min(FFT_B_K1, n1)
    return pl.pallas_call(
        functools.partial(_fft_b_kernel, nk=nk, scale=scale), grid=(B, n1 // nk),
        in_specs=[pl.BlockSpec((1, 2, nk, LANES, F_DIM), lambda b, j: (b, 0, j, 0, 0)),
                  pl.BlockSpec((nk, LANES, 2 * LANES), lambda b, j: (j, 0, 0))],
        out_specs=pl.BlockSpec((1, LANES, nk * F_DIM), lambda b, j: (b, 0, j)),
        out_shape=jax.ShapeDtypeStruct((B, LANES, n1 * F_DIM), BF16),
        compiler_params=_params(("parallel", "parallel")), name="fft_b",
    )(y5, gtab)


def _post_kernel(oT_ref, yf_ref, g_ref, h0_ref, wo_ref, wf_ref, wout_ref, bout_ref, lg_ref, lb_ref, h1_ref, h1T_ref):
    o = oT_ref[0].T.astype(BF16)
    y_a = _dot(o, wo_ref[...])
    y_f = _dot(yf_ref[...], wf_ref[...])
    g = g_ref[...].astype(F32)
    m = (g[:, :D_MODEL] * y_a + g[:, D_MODEL:] * y_f).astype(BF16)
    mix = _dot(m, wout_ref[...]) + bout_ref[...]
    h1 = _layer_norm(DN_ALPHA * h0_ref[...] + mix, lg_ref[...], lb_ref[...])
    h1_ref[...] = h1
    h1T_ref[...] = h1.T.astype(BF16)


def _post(oT, yf, g, h0, B, S, w):
    T = B * S
    tm = TM_POST
    nb = S // tm
    tok = lambda b, i: (b * nb + i, 0)
    return pl.pallas_call(
        _post_kernel, grid=(B, nb),
        in_specs=[pl.BlockSpec((1, N_HEADS * V_DIM, tm), lambda b, i: (b, 0, i)),
                  pl.BlockSpec((tm, F_DIM), tok), pl.BlockSpec((tm, 2 * D_MODEL), tok), pl.BlockSpec((tm, D_MODEL), tok),
                  _const_spec((N_HEADS * V_DIM, D_MODEL)), _const_spec((F_DIM, D_MODEL)), _const_spec((D_MODEL, D_MODEL)),
                  _const_spec((1, D_MODEL)), _const_spec((1, D_MODEL)), _const_spec((1, D_MODEL))],
        out_specs=[pl.BlockSpec((tm, D_MODEL), tok), pl.BlockSpec((D_MODEL, tm), lambda b, i: (0, b * nb + i))],
        out_shape=[jax.ShapeDtypeStruct((T, D_MODEL), F32), jax.ShapeDtypeStruct((D_MODEL, T), BF16)],
        compiler_params=_params(("parallel", "parallel")), name="post",
    )(oT, yf, g, h0, w["wo"], w["wf"], w["wout"], w["bout"], w["ln1_g"], w["ln1_b"])


def _extract_top16(s):
    row = lax.broadcasted_iota(jnp.int32, s.shape, 0).astype(F32)
    slot = lax.broadcasted_iota(jnp.int32, (PEER_TOPK, s.shape[1]), 0)
    rank = jnp.full(s.shape, float(PEER_TOPK), F32)
    vals = jnp.zeros((PEER_TOPK, s.shape[1]), F32)
    for r in range(PEER_TOPK):
        m = jnp.max(s, axis=0, keepdims=True)
        idx = jnp.min(jnp.where(s == m, row, float(PEER_NKEYS)), axis=0, keepdims=True)
        hit = row == idx
        s = jnp.where(hit, -jnp.inf, s)
        rank = jnp.where(hit, float(r), rank)
        vals = jnp.where(slot == r, m, vals)
    return vals, rank


def _cand_rows():
    groups = [[(i, j0 + r) if r < valid else None for r in range(SUBLANES)] for (i, j0, valid) in _CAND_GROUPS]
    groups.append([(SUBLANES + r, 0) for r in range(SUBLANES)])
    return groups


def _static_beats(cp, c):
    if cp == c:
        return 0
    if cp[0] <= c[0] and cp[1] <= c[1]:
        return 1
    if cp[0] >= c[0] and cp[1] >= c[1]:
        return 0
    return None


def _static_counts():
    rows = _cand_rows()
    out = [[0.0] * SUBLANES for _ in rows]
    for cg in rows:
        for cp in cg:
            if cp is None:
                continue
            for g, tgt in enumerate(rows):
                res = [_static_beats(cp, c) if c is not None else 0 for c in tgt]
                if all(v is not None for v in res):
                    for r in range(SUBLANES):
                        out[g][r] += float(res[r])
    return out


def _select_pairs(a_rep, b_lo, b_hi, a_hi, static_counts):
    L = b_lo.shape[1]
    sub = lax.broadcasted_iota(jnp.int32, (SUBLANES, L), 0)
    rows = _cand_rows()
    groups = []
    for (i, j0, valid) in _CAND_GROUPS:
        v = a_rep[i] + (b_lo if j0 == 0 else b_hi)
        if valid < SUBLANES:
            v = jnp.where(sub < valid, v, -jnp.inf)
        groups.append(v)
    groups.append(a_hi + jnp.broadcast_to(b_lo[0:1, :], (SUBLANES, L)))
    n_groups = len(groups)
    counts = [static_counts[g] for g in range(n_groups)]
    for gp in range(n_groups):
        for rp in range(SUBLANES):
            cp = rows[gp][rp]
            if cp is None:
                continue
            vb = jnp.broadcast_to(groups[gp][rp:rp + 1, :], (SUBLANES, L))
            for g in range(n_groups):
                if all(c is None or _static_beats(cp, c) is not None for c in rows[g]):
                    continue
                if g < gp:
                    beats = jnp.where(vb > groups[g], 1.0, 0.0)
                elif g > gp:
                    beats = jnp.where(vb >= groups[g], 1.0, 0.0)
                else:
                    beats = jnp.where(sub > rp, jnp.where(vb >= groups[g], 1.0, 0.0), jnp.where(vb > groups[g], 1.0, 0.0))
                counts[g] = counts[g] + beats
    top = groups[0][0:1, :]
    valid_rows = [g[2] for g in _CAND_GROUPS] + [SUBLANES]
    sel = []
    z = jnp.zeros((1, L), F32)
    for g in range(n_groups):
        s_g = jnp.where(counts[g] < float(PEER_TOPK), 1.0, 0.0)
        if valid_rows[g] < SUBLANES:
            s_g = jnp.where(sub < valid_rows[g], s_g, 0.0)
        sel.append(s_g)
        z = z + jnp.sum(s_g * jnp.exp(groups[g] - top), axis=0, keepdims=True)
    n = [jnp.sum(sel[0] + sel[1], axis=0, keepdims=True)]
    for g in range(2, n_groups - 1):
        n.append(jnp.sum(sel[g], axis=0, keepdims=True))
    for r in range(SUBLANES):
        n.append(sel[n_groups - 1][r:r + 1, :])
    return n, z


def _rows_from_rep(rep, lo):
    sub = lax.broadcasted_iota(jnp.int32, rep[0].shape, 0)
    out = rep[lo]
    for r in range(1, SUBLANES):
        out = jnp.where(sub == r, rep[lo + r], out)
    return out


def _oddeven_merge_pairs(n):
    pairs = []
    t = n.bit_length() - 1
    for pi in range(t):
        p = 1 << pi
        for ki in range(pi, -1, -1):
            k = 1 << ki
            for j in range(k % p, n - k, 2 * k):
                for i in range(min(k, n - j - k)):
                    if (i + j) // (2 * p) == (i + j + k) // (2 * p):
                        pairs.append((i + j, i + j + k))
    return pairs


_SORT16 = _oddeven_merge_pairs(PEER_TOPK)


def _top16_values(s):
    w = [s[g * SUBLANES:(g + 1) * SUBLANES] for g in range(PEER_NKEYS // SUBLANES)]
    for (i, j) in _SORT16:
        w[i], w[j] = jnp.maximum(w[i], w[j]), jnp.minimum(w[i], w[j])
    for shift in (4, 2, 1):
        other = [pltpu.roll(x, shift, axis=0) for x in w]
        w = [jnp.maximum(w[i], other[PEER_TOPK - 1 - i]) for i in range(PEER_TOPK)]
        d = PEER_TOPK // 2
        while d >= 1:
            for i in range(PEER_TOPK):
                if i & d == 0:
                    w[i], w[i + d] = jnp.maximum(w[i], w[i + d]), jnp.minimum(w[i], w[i + d])
            d //= 2
    return w


def _tie_flags(s, w):
    flag = jnp.zeros_like(w[0])
    for r in range(PEER_TOPK - 1):
        flag = flag + jnp.where(w[r] == w[r + 1], 1.0, 0.0)
    parts = [jnp.where(s[g * SUBLANES:(g + 1) * SUBLANES] >= w[PEER_TOPK - 1], 1.0, 0.0)
             for g in range(PEER_NKEYS // SUBLANES)]
    while len(parts) > 1:
        parts = [parts[i] + parts[i + 1] for i in range(0, len(parts), 2)]
    count = jnp.sum(parts[0], axis=0, keepdims=True)
    return flag + (count - float(PEER_TOPK))


def _peer_topk_kernel(h1T_ref, wqT_ref, keys_ref, sc_ref, n_ref, c_ref, r2_ref, e2_ref, qp_ref):
    tm = h1T_ref.shape[1]
    qp_ref[...] = _dot(wqT_ref[...], h1T_ref[...]).astype(BF16)

    def scores(h, lo):
        r0, r1 = 2 * h * PEER_HALF, (2 * h + 1) * PEER_HALF
        if not isinstance(h, int):
            r0, r1 = pl.multiple_of(r0, PEER_HALF), pl.multiple_of(r1, PEER_HALF)
        s1 = _dot(keys_ref[2 * h], qp_ref[pl.ds(r0, PEER_HALF), pl.ds(lo, LANES)])
        s2 = _dot(keys_ref[2 * h + 1], qp_ref[pl.ds(r1, PEER_HALF), pl.ds(lo, LANES)])
        return s1, s2

    def store(h, lo, n_a, c_a, rank2, e2):
        n_ref[h, :, pl.ds(lo, LANES)] = n_a
        c_ref[h, :, pl.ds(lo, LANES)] = c_a
        r2_ref[h, :, pl.ds(lo, LANES)] = rank2.astype(BF16)
        e2_ref[h, :, pl.ds(lo, LANES)] = e2.astype(BF16)

    def fast_head(h, lo):
        s1, s2 = scores(h, lo)
        w1, w2 = _top16_values(s1), _top16_values(s2)
        n, z = _select_pairs(w1, _rows_from_rep(w2, 0), _rows_from_rep(w2, SUBLANES), _rows_from_rep(w1, SUBLANES),
                             sc_ref)
        n_parts, r_parts = [], []
        for g in range(PEER_NKEYS // SUBLANES):
            s1_g, s2_g = s1[g * SUBLANES:(g + 1) * SUBLANES], s2[g * SUBLANES:(g + 1) * SUBLANES]
            n_g = jnp.zeros_like(s1_g)
            r_g = jnp.full_like(s2_g, float(PEER_TOPK))
            for r in range(PEER_TOPK):
                n_g = jnp.where(s1_g == w1[r], n[r], n_g)
                r_g = jnp.where(s2_g == w2[r], float(r), r_g)
            n_parts.append(n_g)
            r_parts.append(r_g)
        c_a = jnp.exp(s1 - w1[0][0:1, :]) * (GELU_HALF / z)
        e2 = jnp.exp(s2 - w2[0][0:1, :])
        store(h, lo, jnp.concatenate(n_parts, axis=0), c_a, jnp.concatenate(r_parts, axis=0), e2)
        return _tie_flags(s1, w1) + _tie_flags(s2, w2)

    def exact_head(h, lo):
        s1, s2 = scores(h, lo)
        a_vals, rank1 = _extract_top16(s1)
        b_vals, rank2 = _extract_top16(s2)
        a_rep = [jnp.broadcast_to(a_vals[i:i + 1, :], (SUBLANES, LANES)) for i in range(PEER_TOPK)]
        n, z = _select_pairs(a_rep, b_vals[:SUBLANES], b_vals[SUBLANES:], a_vals[SUBLANES:], sc_ref)
        n_a = jnp.zeros_like(s1)
        for i in range(PEER_TOPK):
            n_a = n_a + jnp.where(rank1 == float(i), n[i], 0.0)
        c_a = jnp.exp(s1 - a_vals[0:1, :]) * (GELU_HALF / z)
        e2 = jnp.exp(s2 - b_vals[0:1, :])
        store(h, lo, n_a, c_a, rank2, e2)

    def chunk(ci, _):
        lo = pl.multiple_of(ci * LANES, LANES)
        flags = jnp.zeros((SUBLANES, LANES), F32)
        for h in range(PEER_HEADS):
            flags = flags + fast_head(h, lo)

        @pl.when(jnp.max(flags) > 0.0)
        def _():
            def per_head(h, carry):
                exact_head(h, lo)
                return carry
            lax.fori_loop(0, PEER_HEADS, per_head, 0)
        return 0

    lax.fori_loop(0, tm // LANES, chunk, 0)


def _peer_topk(h1T, wqT, keys):
    T = h1T.shape[1]
    tm = TM_TOPK
    nq = wqT.shape[0]
    col = lambda i: (0, i)
    col3 = lambda i: (0, 0, i)
    static = jnp.broadcast_to(jnp.asarray(_static_counts(), F32)[:, :, None], (len(_CAND_GROUPS) + 1, SUBLANES, LANES))
    return pl.pallas_call(
        _peer_topk_kernel, grid=(T // tm,),
        in_specs=[pl.BlockSpec((D_MODEL, tm), col), _const_spec((nq, D_MODEL)),
                  _const_spec((2 * PEER_HEADS, PEER_NKEYS, PEER_HALF)), _const_spec(static.shape)],
        out_specs=[pl.BlockSpec((PEER_HEADS, PEER_NKEYS, tm), col3), pl.BlockSpec((PEER_HEADS, PEER_NKEYS, tm), col3),
                   pl.BlockSpec((PEER_HEADS, PEER_NKEYS, tm), col3), pl.BlockSpec((PEER_HEADS, PEER_NKEYS, tm), col3)],
        out_shape=[jax.ShapeDtypeStruct((PEER_HEADS, PEER_NKEYS, T), F32),
                   jax.ShapeDtypeStruct((PEER_HEADS, PEER_NKEYS, T), F32),
                   jax.ShapeDtypeStruct((PEER_HEADS, PEER_NKEYS, T), BF16),
                   jax.ShapeDtypeStruct((PEER_HEADS, PEER_NKEYS, T), BF16)],
        scratch_shapes=[pltpu.VMEM((nq, tm), BF16)],
        compiler_params=_params(("parallel",)), name="peer_topk",
    )(h1T, wqT, keys, static)


def _peer_dense_kernel(xT_ref, wu_ref, wvT_ref, n_ref, c_ref, r2_ref, e2_ref, h1_ref, lg_ref, lb_ref, out_ref,
                       acc_ref, *, eb):
    j = pl.program_id(1)
    tm = xT_ref.shape[1]

    @pl.when(j == 0)
    def _():
        acc_ref[...] = jnp.zeros_like(acc_ref)

    n_chains = eb // PEER_CHAIN
    rows = [slice(ch * PEER_CHAIN, (ch + 1) * PEER_CHAIN) for ch in range(n_chains)]

    def first_matmul(ch):
        return _dot(wu_ref[rows[ch], :], xT_ref[...])

    def gate_gelu(ch, hpre):
        acts = []
        for aa in range(PEER_CHAIN // PEER_NKEYS):
            a_key = (j * n_chains + ch) * (PEER_CHAIN // PEER_NKEYS) + aa
            gate = jnp.zeros((PEER_NKEYS, tm), BF16)
            for h in range(PEER_HEADS):
                n_b = jnp.broadcast_to(n_ref[h, pl.ds(a_key, 1), :], (PEER_NKEYS, tm)).astype(BF16)
                c_b = jnp.broadcast_to(c_ref[h, pl.ds(a_key, 1), :], (PEER_NKEYS, tm)).astype(BF16)
                gate = gate + jnp.where(r2_ref[h] < n_b, e2_ref[h], jnp.zeros((), BF16)) * c_b
            hp = hpre[aa * PEER_NKEYS:(aa + 1) * PEER_NKEYS, :]
            act = hp * (1.0 + lax.erf(hp * (1.0 / math.sqrt(2.0))))
            acts.append(act.astype(BF16) * gate)
        return jnp.concatenate(acts, axis=0)

    hpre, act = {}, {}
    for t in range(n_chains + 2):
        if t < n_chains:
            hpre[t] = first_matmul(t)
        if 0 <= t - 1 < n_chains:
            act[t - 1] = gate_gelu(t - 1, hpre.pop(t - 1))
        if 0 <= t - 2 < n_chains:
            acc_ref[...] += _dot(wvT_ref[:, rows[t - 2]], act.pop(t - 2))

    @pl.when(j == pl.num_programs(1) - 1)
    def _():
        y = acc_ref[...].T
        out_ref[...] = _layer_norm(DN_ALPHA * h1_ref[...] + y, lg_ref[...], lb_ref[...])


def _peer_dense(h1T, wu, wvT, n_tab, c_tab, r2, e2, h1, lg, lb):
    T = h1T.shape[1]
    tm, eb = TM_PEER, EB_PEER
    col = lambda i, j: (0, i)
    col3 = lambda i, j: (0, 0, i)
    row = lambda i, j: (i, 0)
    return pl.pallas_call(
        functools.partial(_peer_dense_kernel, eb=eb), grid=(T // tm, PEER_EXPERTS // eb),
        in_specs=[pl.BlockSpec((D_MODEL, tm), col),
                  pl.BlockSpec((eb, D_MODEL), lambda i, j: (j, 0)),
                  pl.BlockSpec((D_MODEL, eb), lambda i, j: (0, j)),
                  pl.BlockSpec((PEER_HEADS, PEER_NKEYS, tm), col3), pl.BlockSpec((PEER_HEADS, PEER_NKEYS, tm), col3),
                  pl.BlockSpec((PEER_HEADS, PEER_NKEYS, tm), col3), pl.BlockSpec((PEER_HEADS, PEER_NKEYS, tm), col3),
                  pl.BlockSpec((tm, D_MODEL), row), _const_spec((1, D_MODEL)), _const_spec((1, D_MODEL))],
        out_specs=pl.BlockSpec((tm, D_MODEL), row),
        out_shape=jax.ShapeDtypeStruct((T, D_MODEL), F32),
        scratch_shapes=[pltpu.VMEM((D_MODEL, tm), F32)],
        compiler_params=_params(("parallel", "arbitrary")), name="peer_dense",
    )(h1T, wu, wvT, n_tab, c_tab, r2, e2, h1, lg, lb)


def _pad_cols(w, lo, total):
    return jnp.pad(w, ((0, 0), (lo, total - lo - w.shape[1])))


def _prepare(S, w_in, b_in, q_norm_g, kv_norm_g, w_uq, w_uk, w_uv, w_o_attn, w_fourier, w_out, b_out, ln1_g, ln1_b,
             peer_w_q, peer_keys, peer_w_u, peer_w_v):
    o_kv, o_kr, o_z, o_g = Q_LORA, Q_LORA + KV_LORA, Q_LORA + KV_LORA + QK_ROPE_DIM, Q_LORA + KV_LORA + QK_ROPE_DIM + F_DIM
    row = lambda v: v.reshape(1, -1).astype(F32)
    w = {}
    w["wq"], w["bq"], w["gq"] = w_in[:, :o_kv].astype(BF16), row(b_in[:o_kv]), row(q_norm_g)
    w["wkv"], w["bkv"], w["gkv"] = w_in[:, o_kv:o_kr].astype(BF16), row(b_in[o_kv:o_kr]), row(kv_norm_g)
    wkr, bkr = w_in[:, o_kr:o_z], b_in[o_kr:o_z].reshape(1, -1)
    swap = lambda m: jnp.concatenate([m[:, ROPE_HALF:], m[:, :ROPE_HALF]], axis=1)
    w["wkr"], w["bkr"] = _pad_cols(wkr, QK_NOPE_DIM, HEAD_PAD).astype(BF16), _pad_cols(bkr, QK_NOPE_DIM, HEAD_PAD).astype(F32)
    w["wkrr"] = _pad_cols(swap(wkr), QK_NOPE_DIM, HEAD_PAD).astype(BF16)
    w["bkrr"] = _pad_cols(swap(bkr), QK_NOPE_DIM, HEAD_PAD).astype(F32)
    w["wz"], w["bz"] = w_in[:, o_z:o_g].astype(BF16), row(b_in[o_z:o_g])
    w["wg"], w["bg"] = w_in[:, o_g:].astype(BF16), row(b_in[o_g:])
    wq3 = w_uq.reshape(Q_LORA, N_HEADS, QK_DIM).transpose(1, 0, 2)
    rope3 = wq3[:, :, QK_NOPE_DIM:]
    rope3_sw = jnp.concatenate([rope3[:, :, ROPE_HALF:], rope3[:, :, :ROPE_HALF]], axis=2)
    padh = lambda m: jnp.pad(m, ((0, 0), (0, 0), (0, HEAD_PAD - m.shape[2])))
    w["wqu"] = padh(wq3).astype(BF16)
    w["wqr"] = padh(jnp.concatenate([jnp.zeros_like(wq3[:, :, :QK_NOPE_DIM]), rope3_sw], axis=2)).astype(BF16)
    wk3 = w_uk.reshape(KV_LORA, N_HEADS, QK_NOPE_DIM).transpose(1, 0, 2)
    w["wku"] = padh(wk3).astype(BF16)
    w["wvu"] = w_uv.astype(BF16)
    pos = jnp.arange(S, dtype=F32)
    inv = 1.0 / (ROPE_THETA ** (jnp.arange(0, QK_ROPE_DIM, 2, dtype=F32) / QK_ROPE_DIM))
    ang = pos[:, None] * inv[None, :]
    cos, sin = jnp.cos(ang), jnp.sin(ang)
    ones, zeros = jnp.ones((S, QK_NOPE_DIM), F32), jnp.zeros((S, QK_NOPE_DIM), F32)
    tail = jnp.zeros((S, HEAD_PAD - QK_DIM), F32)
    scale = QK_DIM ** -0.5 * math.log2(math.e)
    w["cos_q"] = jnp.concatenate([ones, cos, cos, tail], axis=1) * scale
    w["sin_q"] = jnp.concatenate([zeros, -sin, sin, tail], axis=1) * scale
    w["cos_k"] = jnp.concatenate([zeros, cos, cos, tail], axis=1)
    w["sin_k"] = jnp.concatenate([zeros, -sin, sin, tail], axis=1)
    n1 = S // LANES
    idx = lambda n: jnp.arange(n, dtype=jnp.int32)
    ang_c = (2.0 * math.pi / F_GROUP_DIM) * ((idx(F_GROUP_DIM)[:, None] * idx(F_GROUP_DIM)[None, :]) % F_GROUP_DIM).astype(F32)
    w["cs128"] = jnp.concatenate([jnp.cos(ang_c), jnp.sin(ang_c)], axis=1).astype(BF16)
    ang_1 = (2.0 * math.pi / n1) * ((idx(n1)[:, None] * idx(n1)[None, :]) % n1).astype(F32)
    c1, s1 = jnp.cos(ang_1), jnp.sin(ang_1)
    w["m1"] = jnp.concatenate([jnp.concatenate([c1, s1], axis=1), jnp.concatenate([-s1, c1], axis=1)], axis=0).astype(BF16)
    kk = idx(n1)[:, None, None] + n1 * idx(LANES)[None, :, None]
    ang_g = (2.0 * math.pi / S) * ((kk * idx(LANES)[None, None, :]) % S).astype(F32)
    w["gtab"] = jnp.concatenate([jnp.cos(ang_g), jnp.sin(ang_g)], axis=2).astype(BF16)
    w["wo"], w["wf"], w["wout"] = w_o_attn.astype(BF16), w_fourier.astype(BF16), w_out.astype(BF16)
    w["bout"], w["ln1_g"], w["ln1_b"] = row(b_out), row(ln1_g), row(ln1_b)
    w["wqT"] = peer_w_q.astype(BF16).T
    w["keys"] = peer_keys.reshape(2 * PEER_HEADS, PEER_NKEYS, PEER_HALF).astype(BF16)
    w["wu"] = peer_w_u.astype(BF16)
    w["wvT"] = peer_w_v.astype(BF16).T
    return w


def kernel(x, ln0_g, ln0_b, w_in, b_in, q_norm_g, kv_norm_g, w_uq, w_uk, w_uv, w_o_attn, w_fourier, w_out, b_out, ln1_g,
           ln1_b, peer_w_q, peer_keys, peer_w_u, peer_w_v, ln2_g, ln2_b):
    B, S, D = x.shape
    assert D == D_MODEL and w_in.shape[0] == DEPTH
    T = B * S
    n1 = S // LANES
    assert S % max(TM_FRONT, TQ_ATTN, TM_POST, LANES * SUBLANES) == 0 and T % max(TM_PEER, TM_TOPK) == 0
    row = lambda v: v.reshape(1, -1).astype(F32)
    h = x.reshape(T, D)
    w = _prepare(S, w_in[0], b_in[0], q_norm_g[0], kv_norm_g[0], w_uq[0], w_uk[0], w_uv[0], w_o_attn[0], w_fourier[0],
                 w_out[0], b_out[0], ln1_g[0], ln1_b[0], peer_w_q[0], peer_keys[0], peer_w_u[0], peer_w_v[0])
    h0, qT, k, vT, ur, ui, g = _front(h, B, S, row(ln0_g), row(ln0_b), w)
    oT = _attention(qT, k, vT)
    y = _fft_a(ur.reshape(B, n1, LANES * F_DIM), ui.reshape(B, n1, LANES * F_DIM), w["m1"])
    yf = _fft_b(y.reshape(B, 2, n1, LANES, F_DIM), w["gtab"], 1.0 / math.sqrt(S * F_GROUP_DIM))
    h1, h1T = _post(oT, yf.reshape(T, F_DIM), g, h0, B, S, w)
    n_tab, c_tab, r2, e2 = _peer_topk(h1T, w["wqT"], w["keys"])
    out = _peer_dense(h1T, w["wu"], w["wvT"], n_tab, c_tab, r2, e2, h1, row(ln2_g[0]), row(ln2_b[0]))
    return out.reshape(B, S, D)
```

```python
import functools
import math

import jax
import jax.numpy as jnp
from jax import lax
from jax.experimental import pallas as pl
from jax.experimental.pallas import tpu as pltpu

F32 = jnp.float32
BF16 = jnp.bfloat16

D_MODEL = 1024
N_HEADS = 8
QK_NOPE_DIM = 64
QK_ROPE_DIM = 32
ROPE_HALF = QK_ROPE_DIM // 2
QK_DIM = QK_NOPE_DIM + QK_ROPE_DIM
V_DIM = 64
Q_LORA = 256
KV_LORA = 256
ROPE_THETA = 10000.0
F_GROUPS = 4
F_GROUP_DIM = 128
F_DIM = F_GROUPS * F_GROUP_DIM
PEER_HEADS = 8
PEER_NKEYS = 128
PEER_EXPERTS = PEER_NKEYS * PEER_NKEYS
PEER_HALF = 128
PEER_TOPK = 16
DEPTH = 1
DN_ALPHA = (2.0 * DEPTH) ** 0.25
LN_EPS = 1e-5
RMS_EPS = 1e-6
GELU_HALF = 0.5

LANES = 128
SUBLANES = 8
HEAD_PAD = LANES
VMEM_LIMIT_BYTES = 56 * 1024 * 1024

TM_FRONT = 256
TQ_ATTN = 2048
TQ_SUB = 256
TK_ATTN = 512
FFT_A_LANES = 8192
FFT_B_K1 = 16
TM_POST = 512
TM_TOPK = 256
TM_PEER = 512
EB_PEER = 2048
PEER_CHAIN = 512

_CAND_GROUPS = ((0, 0, 8), (0, 8, 8), (1, 0, 8), (2, 0, 5), (3, 0, 4), (4, 0, 3), (5, 0, 2), (6, 0, 2), (7, 0, 2))


def _params(sem):
    return pltpu.CompilerParams(dimension_semantics=sem, vmem_limit_bytes=VMEM_LIMIT_BYTES)


def _const_spec(shape):
    nd = len(shape)
    return pl.BlockSpec(shape, lambda *_: (0,) * nd)


def _layer_norm(x, g, b):
    mu = jnp.mean(x, axis=-1, keepdims=True)
    xc = x - mu
    var = jnp.mean(xc * xc, axis=-1, keepdims=True)
    return xc * lax.rsqrt(var + LN_EPS) * g + b


def _rms_norm(x, g):
    return x * lax.rsqrt(jnp.mean(x * x, axis=-1, keepdims=True) + RMS_EPS) * g


def _dot(a, b):
    return jnp.dot(a, b, preferred_element_type=F32)


def _front_kernel(x_ref, g0_ref, b0_ref, wq_ref, bq_ref, gq_ref, wqu_ref, wqr_ref, wkv_ref, bkv_ref, gkv_ref, wku_ref,
                  wvu_ref, wkr_ref, bkr_ref, cq_ref, sq_ref, ck_ref, sk_ref,
                  wz_ref, bz_ref, cs_ref, wg_ref, bg_ref,
                  h0_ref, qT_ref, k_ref, vT_ref, ur_ref, ui_ref, g_ref):
    h0 = _layer_norm(x_ref[...], g0_ref[...], b0_ref[...])
    h0_ref[...] = h0
    hb = h0.astype(BF16)
    cq = _rms_norm(_dot(hb, wq_ref[...]) + bq_ref[...], gq_ref[...]).astype(BF16)
    ckv = _rms_norm(_dot(hb, wkv_ref[...]) + bkv_ref[...], gkv_ref[...]).astype(BF16)
    cos_q, sin_q = cq_ref[...], sq_ref[...]
    kr2 = _dot(hb, wkr_ref[...]) + bkr_ref[...]
    k_rope = kr2[:, :HEAD_PAD] * ck_ref[...] + kr2[:, HEAD_PAD:] * sk_ref[...]
    cos_q2, sin_q2 = jnp.concatenate([cos_q, cos_q], axis=1), jnp.concatenate([sin_q, sin_q], axis=1)
    k_rope2 = jnp.concatenate([k_rope, k_rope], axis=1)
    for hp in range(N_HEADS // 2):
        q_p = _dot(cq, wqu_ref[hp]) * cos_q2 + _dot(cq, wqr_ref[hp]) * sin_q2
        k_p = (_dot(ckv, wku_ref[hp]) + k_rope2).astype(BF16)
        for hh in range(2):
            h = 2 * hp + hh
            lanes = slice(hh * HEAD_PAD, (hh + 1) * HEAD_PAD)
            qT_ref[0, h * HEAD_PAD:(h + 1) * HEAD_PAD, :] = q_p[:, lanes].T.astype(BF16)
            k_ref[0, h] = k_p[:, lanes]
    v_all = _dot(ckv, wvu_ref[...])
    vT_ref[0] = v_all.T.astype(BF16)
    z = (_dot(hb, wz_ref[...]) + bz_ref[...]).astype(BF16)
    cs = cs_ref[...]
    for gi in range(F_GROUPS):
        lo, hi = gi * F_GROUP_DIM, (gi + 1) * F_GROUP_DIM
        pq = _dot(z[:, lo:hi], cs)
        ur_ref[:, lo:hi] = pq[:, :F_GROUP_DIM].astype(BF16)
        ui_ref[:, lo:hi] = (-pq[:, F_GROUP_DIM:]).astype(BF16)
    g_ref[...] = jax.nn.sigmoid(_dot(hb, wg_ref[...]) + bg_ref[...]).astype(BF16)


def _front(x2, B, S, g0, b0, w):
    T = B * S
    tm = TM_FRONT
    nb = S // tm
    tok = lambda b, i: (b * nb + i, 0)
    pos = lambda b, i: (i, 0)
    in_specs = [
        pl.BlockSpec((tm, D_MODEL), tok), _const_spec((1, D_MODEL)), _const_spec((1, D_MODEL)),
        _const_spec((D_MODEL, Q_LORA)), _const_spec((1, Q_LORA)), _const_spec((1, Q_LORA)),
        _const_spec((N_HEADS // 2, Q_LORA, 2 * HEAD_PAD)), _const_spec((N_HEADS // 2, Q_LORA, 2 * HEAD_PAD)),
        _const_spec((D_MODEL, KV_LORA)), _const_spec((1, KV_LORA)), _const_spec((1, KV_LORA)),
        _const_spec((N_HEADS // 2, KV_LORA, 2 * HEAD_PAD)), _const_spec((KV_LORA, N_HEADS * V_DIM)),
        _const_spec((D_MODEL, 2 * HEAD_PAD)), _const_spec((1, 2 * HEAD_PAD)),
        pl.BlockSpec((tm, HEAD_PAD), pos), pl.BlockSpec((tm, HEAD_PAD), pos),
        pl.BlockSpec((tm, HEAD_PAD), pos), pl.BlockSpec((tm, HEAD_PAD), pos),
        _const_spec((D_MODEL, F_DIM)), _const_spec((1, F_DIM)), _const_spec((F_GROUP_DIM, 2 * F_GROUP_DIM)),
        _const_spec((D_MODEL, 2 * D_MODEL)), _const_spec((1, 2 * D_MODEL)),
    ]
    out_specs = [
        pl.BlockSpec((tm, D_MODEL), tok),
        pl.BlockSpec((1, N_HEADS * HEAD_PAD, tm), lambda b, i: (b, 0, i)),
        pl.BlockSpec((1, N_HEADS, tm, HEAD_PAD), lambda b, i: (b, 0, i, 0)),
        pl.BlockSpec((1, N_HEADS * V_DIM, tm), lambda b, i: (b, 0, i)),
        pl.BlockSpec((tm, F_DIM), tok), pl.BlockSpec((tm, F_DIM), tok), pl.BlockSpec((tm, 2 * D_MODEL), tok),
    ]
    out_shape = [
        jax.ShapeDtypeStruct((T, D_MODEL), F32),
        jax.ShapeDtypeStruct((B, N_HEADS * HEAD_PAD, S), BF16),
        jax.ShapeDtypeStruct((B, N_HEADS, S, HEAD_PAD), BF16),
        jax.ShapeDtypeStruct((B, N_HEADS * V_DIM, S), BF16),
        jax.ShapeDtypeStruct((T, F_DIM), BF16), jax.ShapeDtypeStruct((T, F_DIM), BF16),
        jax.ShapeDtypeStruct((T, 2 * D_MODEL), BF16),
    ]
    return pl.pallas_call(
        _front_kernel, grid=(B, nb), in_specs=in_specs, out_specs=out_specs, out_shape=out_shape,
        compiler_params=_params(("parallel", "parallel")), name="front",
    )(x2, g0, b0, w["wq"], w["bq"], w["gq"], w["wqu"], w["wqr"], w["wkv"], w["bkv"], w["gkv"], w["wku"], w["wvu"],
      w["wkr"], w["bkr"], w["cos_q"], w["sin_q"], w["cos_k"], w["sin_k"],
      w["wz"], w["bz"], w["cs128"], w["wg"], w["bg"])


def _col_reduce(x, op, final):
    parts = [x[i:i + SUBLANES] for i in range(0, x.shape[0], SUBLANES)]
    while len(parts) > 1:
        parts = [op(parts[i], parts[i + 1]) for i in range(0, len(parts), 2)]
    return final(parts[0], axis=0, keepdims=True)


def _attn_kernel(qT_ref, k_ref, vT_ref, oT_ref, s_ref, *, tk, n_sub):
    tq = qT_ref.shape[2]
    tsub = tq // n_sub
    n_chunks = k_ref.shape[2] // tk
    subs = range(n_sub)

    def scores(c, slot):
        off = pl.multiple_of(c * tk, tk)
        k_c = k_ref[0, 0, pl.ds(off, tk), :]
        cmax = []
        for u in subs:
            s = _dot(k_c, qT_ref[0, :, u * tsub:(u + 1) * tsub])
            s_ref[slot, u] = s
            cmax.append(_col_reduce(s, jnp.maximum, jnp.max))
        return tuple(cmax)

    def softmax_pv(c, slot, carry, cmax):
        off = pl.multiple_of(c * tk, tk)
        vT_c = vT_ref[0, :, pl.ds(off, tk)]
        m_new = [jnp.maximum(carry[u][0], cmax[u]) for u in subs]
        p = [jnp.exp2(s_ref[slot, u] - m_new[u]) for u in subs]
        alpha = [jnp.exp2(carry[u][0] - m_new[u]) for u in subs]
        l = [alpha[u] * carry[u][1] + _col_reduce(p[u], jnp.add, jnp.sum) for u in subs]
        acc = [alpha[u] * carry[u][2] + _dot(vT_c, p[u].astype(BF16)) for u in subs]
        return tuple((m_new[u], l[u], acc[u]) for u in subs)

    def body(i, carry):
        state, cmax0 = carry
        c0 = 2 * i
        cmax1 = scores(c0 + 1, 1)
        state = softmax_pv(c0, 0, state, cmax0)
        cmax0 = scores(c0 + 2, 0)
        return softmax_pv(c0 + 1, 1, state, cmax1), cmax0

    cmax_first = scores(0, 0)
    init = tuple((jnp.full((1, tsub), -jnp.inf, F32), jnp.zeros((1, tsub), F32), jnp.zeros((V_DIM, tsub), F32))
                 for _ in range(n_sub))
    state, cmax0 = lax.fori_loop(0, n_chunks // 2 - 1, body, (init, cmax_first))
    cmax1 = scores(n_chunks - 1, 1)
    state = softmax_pv(n_chunks - 2, 0, state, cmax0)
    fin = softmax_pv(n_chunks - 1, 1, state, cmax1)
    for u in range(n_sub):
        _, l, acc = fin[u]
        oT_ref[0, :, u * tsub:(u + 1) * tsub] = acc * (1.0 / l)


def _attention(qT, k, vT):
    B, _, S = qT.shape
    tq = TQ_ATTN
    return pl.pallas_call(
        functools.partial(_attn_kernel, tk=min(TK_ATTN, S), n_sub=TQ_ATTN // TQ_SUB),
        grid=(B, N_HEADS, S // tq),
        in_specs=[
            pl.BlockSpec((1, HEAD_PAD, tq), lambda b, h, i: (b, h, i)),
            pl.BlockSpec((1, 1, S, HEAD_PAD), lambda b, h, i: (b, h, 0, 0)),
            pl.BlockSpec((1, V_DIM, S), lambda b, h, i: (b, h, 0)),
        ],
        out_specs=pl.BlockSpec((1, V_DIM, tq), lambda b, h, i: (b, h, i)),
        out_shape=jax.ShapeDtypeStruct((B, N_HEADS * V_DIM, S), F32),
        scratch_shapes=[pltpu.VMEM((2, TQ_ATTN // TQ_SUB, min(TK_ATTN, S), TQ_SUB), F32)],
        compiler_params=_params(("parallel", "parallel", "parallel")),
        name="attention",
    )(qT, k, vT)


def _fft_a_kernel(ur_ref, ui_ref, m1_ref, y_ref):
    u = jnp.concatenate([ur_ref[0], ui_ref[0]], axis=0)
    y_ref[0] = _dot(m1_ref[...], u).astype(BF16)


def _fft_a(ur3, ui3, m1):
    B, n1, W = ur3.shape
    L = min(FFT_A_LANES, W)
    blk = lambda b, j: (b, 0, j)
    return pl.pallas_call(
        _fft_a_kernel, grid=(B, W // L),
        in_specs=[pl.BlockSpec((1, n1, L), blk), pl.BlockSpec((1, n1, L), blk), _const_spec((2 * n1, 2 * n1))],
        out_specs=pl.BlockSpec((1, 2 * n1, L), blk),
        out_shape=jax.ShapeDtypeStruct((B, 2 * n1, W), BF16),
        compiler_params=_params(("parallel", "parallel")), name="fft_a",
    )(ur3, ui3, m1)


def _fft_b_kernel(y_ref, g_ref, o_ref, *, nk, scale):
    for j in range(nk):
        ycat = jnp.concatenate([y_ref[0, 0, j], y_ref[0, 1, j]], axis=0)
        o_ref[0, :, j * F_DIM:(j + 1) * F_DIM] = (_dot(g_ref[j], ycat) * scale).astype(BF16)


def _fft_b(y5, gtab, scale):
    B, _, n1, _, _ = y5.shape
    nk = min(FFT_B_K1, n1)
    return pl.pallas_call(
        functools.partial(_fft_b_kernel, nk=nk, scale=scale), grid=(B, n1 // nk),
        in_specs=[pl.BlockSpec((1, 2, nk, LANES, F_DIM), lambda b, j: (b, 0, j, 0, 0)),
                  pl.BlockSpec((nk, LANES, 2 * LANES), lambda b, j: (j, 0, 0))],
        out_specs=pl.BlockSpec((1, LANES, nk * F_DIM), lambda b, j: (b, 0, j)),
        out_shape=jax.ShapeDtypeStruct((B, LANES, n1 * F_DIM), BF16),
        compiler_params=_params(("parallel", "parallel")), name="fft_b",
    )(y5, gtab)


def _post_kernel(oT_ref, yf_ref, g_ref, h0_ref, wo_ref, wf_ref, wout_ref, bout_ref, lg_ref, lb_ref, h1_ref, h1T_ref):
    o = oT_ref[0].T.astype(BF16)
    y_a = _dot(o, wo_ref[...])
    y_f = _dot(yf_ref[...], wf_ref[...])
    g = g_ref[...].astype(F32)
    m = (g[:, :D_MODEL] * y_a + g[:, D_MODEL:] * y_f).astype(BF16)
    mix = _dot(m, wout_ref[...]) + bout_ref[...]
    h1 = _layer_norm(DN_ALPHA * h0_ref[...] + mix, lg_ref[...], lb_ref[...])
    h1_ref[...] = h1
    h1T_ref[...] = h1.T.astype(BF16)


def _post(oT, yf, g, h0, B, S, w):
    T = B * S
    tm = TM_POST
    nb = S // tm
    tok = lambda b, i: (b * nb + i, 0)
    return pl.pallas_call(
        _post_kernel, grid=(B, nb),
        in_specs=[pl.BlockSpec((1, N_HEADS * V_DIM, tm), lambda b, i: (b, 0, i)),
                  pl.BlockSpec((tm, F_DIM), tok), pl.BlockSpec((tm, 2 * D_MODEL), tok), pl.BlockSpec((tm, D_MODEL), tok),
                  _const_spec((N_HEADS * V_DIM, D_MODEL)), _const_spec((F_DIM, D_MODEL)), _const_spec((D_MODEL, D_MODEL)),
                  _const_spec((1, D_MODEL)), _const_spec((1, D_MODEL)), _const_spec((1, D_MODEL))],
        out_specs=[pl.BlockSpec((tm, D_MODEL), tok), pl.BlockSpec((D_MODEL, tm), lambda b, i: (0, b * nb + i))],
        out_shape=[jax.ShapeDtypeStruct((T, D_MODEL), F32), jax.ShapeDtypeStruct((D_MODEL, T), BF16)],
        compiler_params=_params(("parallel", "parallel")), name="post",
    )(oT, yf, g, h0, w["wo"], w["wf"], w["wout"], w["bout"], w["ln1_g"], w["ln1_b"])


def _extract_top16(s):
    row = lax.broadcasted_iota(jnp.int32, s.shape, 0).astype(F32)
    slot = lax.broadcasted_iota(jnp.int32, (PEER_TOPK, s.shape[1]), 0)
    rank = jnp.full(s.shape, float(PEER_TOPK), F32)
    vals = jnp.zeros((PEER_TOPK, s.shape[1]), F32)
    for r in range(PEER_TOPK):
        m = jnp.max(s, axis=0, keepdims=True)
        idx = jnp.min(jnp.where(s == m, row, float(PEER_NKEYS)), axis=0, keepdims=True)
        hit = row == idx
        s = jnp.where(hit, -jnp.inf, s)
        rank = jnp.where(hit, float(r), rank)
        vals = jnp.where(slot == r, m, vals)
    return vals, rank


def _cand_rows():
    groups = [[(i, j0 + r) if r < valid else None for r in range(SUBLANES)] for (i, j0, valid) in _CAND_GROUPS]
    groups.append([(SUBLANES + r, 0) for r in range(SUBLANES)])
    return groups


def _static_beats(cp, c):
    if cp == c:
        return 0
    if cp[0] <= c[0] and cp[1] <= c[1]:
        return 1
    if cp[0] >= c[0] and cp[1] >= c[1]:
        return 0
    return None


def _static_counts():
    rows = _cand_rows()
    out = [[0.0] * SUBLANES for _ in rows]
    for cg in rows:
        for cp in cg:
            if cp is None:
                continue
            for g, tgt in enumerate(rows):
                res = [_static_beats(cp, c) if c is not None else 0 for c in tgt]
                if all(v is not None for v in res):
                    for r in range(SUBLANES):
                        out[g][r] += float(res[r])
    return out


def _select_pairs(a_rep, b_lo, b_hi, a_hi, static_counts):
    L = b_lo.shape[1]
    sub = lax.broadcasted_iota(jnp.int32, (SUBLANES, L), 0)
    rows = _cand_rows()
    groups = []
    for (i, j0, valid) in _CAND_GROUPS:
        v = a_rep[i] + (b_lo if j0 == 0 else b_hi)
        if valid < SUBLANES:
            v = jnp.where(sub < valid, v, -jnp.inf)
        groups.append(v)
    groups.append(a_hi + jnp.broadcast_to(b_lo[0:1, :], (SUBLANES, L)))
    n_groups = len(groups)
    counts = [static_counts[g] for g in range(n_groups)]
    for gp in range(n_groups):
        for rp in range(SUBLANES):
            cp = rows[gp][rp]
            if cp is None:
                continue
            vb = jnp.broadcast_to(groups[gp][rp:rp + 1, :], (SUBLANES, L))
            for g in range(n_groups):
                if all(c is None or _static_beats(cp, c) is not None for c in rows[g]):
                    continue
                if g < gp:
                    beats = jnp.where(vb > groups[g], 1.0, 0.0)
                elif g > gp:
                    beats = jnp.where(vb >= groups[g], 1.0, 0.0)
                else:
                    beats = jnp.where(sub > rp, jnp.where(vb >= groups[g], 1.0, 0.0), jnp.where(vb > groups[g], 1.0, 0.0))
                counts[g] = counts[g] + beats
    top = groups[0][0:1, :]
    valid_rows = [g[2] for g in _CAND_GROUPS] + [SUBLANES]
    sel = []
    z = jnp.zeros((1, L), F32)
    for g in range(n_groups):
        s_g = jnp.where(counts[g] < float(PEER_TOPK), 1.0, 0.0)
        if valid_rows[g] < SUBLANES:
            s_g = jnp.where(sub < valid_rows[g], s_g, 0.0)
        sel.append(s_g)
        z = z + jnp.sum(s_g * jnp.exp(groups[g] - top), axis=0, keepdims=True)
    n = [jnp.sum(sel[0] + sel[1], axis=0, keepdims=True)]
    for g in range(2, n_groups - 1):
        n.append(jnp.sum(sel[g], axis=0, keepdims=True))
    for r in range(SUBLANES):
        n.append(sel[n_groups - 1][r:r + 1, :])
    return n, z


def _rows_from_rep(rep, lo):
    sub = lax.broadcasted_iota(jnp.int32, rep[0].shape, 0)
    out = rep[lo]
    for r in range(1, SUBLANES):
        out = jnp.where(sub == r, rep[lo + r], out)
    return out


def _oddeven_merge_pairs(n):
    pairs = []
    t = n.bit_length() - 1
    for pi in range(t):
        p = 1 << pi
        for ki in range(pi, -1, -1):
            k = 1 << ki
            for j in range(k % p, n - k, 2 * k):
                for i in range(min(k, n - j - k)):
                    if (i + j) // (2 * p) == (i + j + k) // (2 * p):
                        pairs.append((i + j, i + j + k))
    return pairs


_SORT16 = _oddeven_merge_pairs(PEER_TOPK)


def _top16_values(s):
    w = [s[g * SUBLANES:(g + 1) * SUBLANES] for g in range(PEER_NKEYS // SUBLANES)]
    for (i, j) in _SORT16:
        w[i], w[j] = jnp.maximum(w[i], w[j]), jnp.minimum(w[i], w[j])
    for shift in (4, 2, 1):
        other = [pltpu.roll(x, shift, axis=0) for x in w]
        w = [jnp.maximum(w[i], other[PEER_TOPK - 1 - i]) for i in range(PEER_TOPK)]
        d = PEER_TOPK // 2
        while d >= 1:
            for i in range(PEER_TOPK):
                if i & d == 0:
                    w[i], w[i + d] = jnp.maximum(w[i], w[i + d]), jnp.minimum(w[i], w[i + d])
            d //= 2
    return w


def _tie_flags(s, w):
    flag = jnp.zeros_like(w[0])
    for r in range(PEER_TOPK - 1):
        flag = flag + jnp.where(w[r] == w[r + 1], 1.0, 0.0)
    parts = [jnp.where(s[g * SUBLANES:(g + 1) * SUBLANES] >= w[PEER_TOPK - 1], 1.0, 0.0)
             for g in range(PEER_NKEYS // SUBLANES)]
    while len(parts) > 1:
        parts = [parts[i] + parts[i + 1] for i in range(0, len(parts), 2)]
    count = jnp.sum(parts[0], axis=0, keepdims=True)
    return flag + (count - float(PEER_TOPK))


def _peer_topk_kernel(h1T_ref, wqT_ref, keys_ref, sc_ref, n_ref, c_ref, r2_ref, e2_ref, qp_ref):
    tm = h1T_ref.shape[1]
    qp_ref[...] = _dot(wqT_ref[...], h1T_ref[...]).astype(BF16)

    def scores(h, lo):
        r0, r1 = 2 * h * PEER_HALF, (2 * h + 1) * PEER_HALF
        if not isinstance(h, int):
            r0, r1 = pl.multiple_of(r0, PEER_HALF), pl.multiple_of(r1, PEER_HALF)
        s1 = _dot(keys_ref[2 * h], qp_ref[pl.ds(r0, PEER_HALF), pl.ds(lo, LANES)])
        s2 = _dot(keys_ref[2 * h + 1], qp_ref[pl.ds(r1, PEER_HALF), pl.ds(lo, LANES)])
        return s1, s2

    def store(h, lo, n_a, c_a, rank2, e2):
        n_ref[h, :, pl.ds(lo, LANES)] = n_a
        c_ref[h, :, pl.ds(lo, LANES)] = c_a
        r2_ref[h, :, pl.ds(lo, LANES)] = rank2.astype(BF16)
        e2_ref[h, :, pl.ds(lo, LANES)] = e2.astype(BF16)

    def fast_head(h, lo):
        s1, s2 = scores(h, lo)
        w1, w2 = _top16_values(s1), _top16_values(s2)
        n, z = _select_pairs(w1, _rows_from_rep(w2, 0), _rows_from_rep(w2, SUBLANES), _rows_from_rep(w1, SUBLANES),
                             sc_ref)
        n_parts, r_parts = [], []
        for g in range(PEER_NKEYS // SUBLANES):
            s1_g, s2_g = s1[g * SUBLANES:(g + 1) * SUBLANES], s2[g * SUBLANES:(g + 1) * SUBLANES]
            n_g = jnp.zeros_like(s1_g)
            r_g = jnp.full_like(s2_g, float(PEER_TOPK))
            for r in range(PEER_TOPK):
                n_g = jnp.where(s1_g == w1[r], n[r], n_g)
                r_g = jnp.where(s2_g == w2[r], float(r), r_g)
            n_parts.append(n_g)
            r_parts.append(r_g)
        c_a = jnp.exp(s1 - w1[0][0:1, :]) * (GELU_HALF / z)
        e2 = jnp.exp(s2 - w2[0][0:1, :])
        store(h, lo, jnp.concatenate(n_parts, axis=0), c_a, jnp.concatenate(r_parts, axis=0), e2)
        return _tie_flags(s1, w1) + _tie_flags(s2, w2)

    def exact_head(h, lo):
        s1, s2 = scores(h, lo)
        a_vals, rank1 = _extract_top16(s1)
        b_vals, rank2 = _extract_top16(s2)
        a_rep = [jnp.broadcast_to(a_vals[i:i + 1, :], (SUBLANES, LANES)) for i in range(PEER_TOPK)]
        n, z = _select_pairs(a_rep, b_vals[:SUBLANES], b_vals[SUBLANES:], a_vals[SUBLANES:], sc_ref)
        n_a = jnp.zeros_like(s1)
        for i in range(PEER_TOPK):
            n_a = n_a + jnp.where(rank1 == float(i), n[i], 0.0)
        c_a = jnp.exp(s1 - a_vals[0:1, :]) * (GELU_HALF / z)
        e2 = jnp.exp(s2 - b_vals[0:1, :])
        store(h, lo, n_a, c_a, rank2, e2)

    def chunk(ci, _):
        lo = pl.multiple_of(ci * LANES, LANES)
        flags = jnp.zeros((SUBLANES, LANES), F32)
        for h in range(PEER_HEADS):
            flags = flags + fast_head(h, lo)

        @pl.when(jnp.max(flags) > 0.0)
        def _():
            def per_head(h, carry):
                exact_head(h, lo)
                return carry
            lax.fori_loop(0, PEER_HEADS, per_head, 0)
        return 0

    lax.fori_loop(0, tm // LANES, chunk, 0)


def _peer_topk(h1T, wqT, keys):
    T = h1T.shape[1]
    tm = TM_TOPK
    nq = wqT.shape[0]
    col = lambda i: (0, i)
    col3 = lambda i: (0, 0, i)
    static = jnp.broadcast_to(jnp.asarray(_static_counts(), F32)[:, :, None], (len(_CAND_GROUPS) + 1, SUBLANES, LANES))
    return pl.pallas_call(
        _peer_topk_kernel, grid=(T // tm,),
        in_specs=[pl.BlockSpec((D_MODEL, tm), col), _const_spec((nq, D_MODEL)),
                  _const_spec((2 * PEER_HEADS, PEER_NKEYS, PEER_HALF)), _const_spec(static.shape)],
        out_specs=[pl.BlockSpec((PEER_HEADS, PEER_NKEYS, tm), col3), pl.BlockSpec((PEER_HEADS, PEER_NKEYS, tm), col3),
                   pl.BlockSpec((PEER_HEADS, PEER_NKEYS, tm), col3), pl.BlockSpec((PEER_HEADS, PEER_NKEYS, tm), col3)],
        out_shape=[jax.ShapeDtypeStruct((PEER_HEADS, PEER_NKEYS, T), F32),
                   jax.ShapeDtypeStruct((PEER_HEADS, PEER_NKEYS, T), F32),
                   jax.ShapeDtypeStruct((PEER_HEADS, PEER_NKEYS, T), BF16),
                   jax.ShapeDtypeStruct((PEER_HEADS, PEER_NKEYS, T), BF16)],
        scratch_shapes=[pltpu.VMEM((nq, tm), BF16)],
        compiler_params=_params(("parallel",)), name="peer_topk",
    )(h1T, wqT, keys, static)


def _peer_dense_kernel(xT_ref, wu_ref, wv_ref, n_ref, c_ref, r2_ref, e2_ref, h1_ref, lg_ref, lb_ref, out_ref,
                       acc_ref, *, eb):
    j = pl.program_id(1)
    tm = xT_ref.shape[1]

    @pl.when(j == 0)
    def _():
        acc_ref[...] = jnp.zeros_like(acc_ref)

    n_chains = eb // PEER_CHAIN
    rows = [slice(ch * PEER_CHAIN, (ch + 1) * PEER_CHAIN) for ch in range(n_chains)]

    def first_matmul(ch):
        return _dot(wu_ref[rows[ch], :], xT_ref[...])

    def gate_gelu(ch, hpre):
        acts = []
        for aa in range(PEER_CHAIN // PEER_NKEYS):
            a_key = (j * n_chains + ch) * (PEER_CHAIN // PEER_NKEYS) + aa
            gate = jnp.zeros((PEER_NKEYS, tm), BF16)
            for h in range(PEER_HEADS):
                n_b = jnp.broadcast_to(n_ref[h, pl.ds(a_key, 1), :], (PEER_NKEYS, tm)).astype(BF16)
                c_b = jnp.broadcast_to(c_ref[h, pl.ds(a_key, 1), :], (PEER_NKEYS, tm)).astype(BF16)
                gate = gate + jnp.where(r2_ref[h] < n_b, e2_ref[h], jnp.zeros((), BF16)) * c_b
            hp = hpre[aa * PEER_NKEYS:(aa + 1) * PEER_NKEYS, :]
            act = hp * (1.0 + lax.erf(hp * (1.0 / math.sqrt(2.0))))
            acts.append(act.astype(BF16) * gate)
        return jnp.concatenate(acts, axis=0)

    hpre, act = {}, {}
    for t in range(n_chains + 2):
        if t < n_chains:
            hpre[t] = first_matmul(t)
        if 0 <= t - 1 < n_chains:
            act[t - 1] = gate_gelu(t - 1, hpre.pop(t - 1))
        if 0 <= t - 2 < n_chains:
            acc_ref[...] += lax.dot_general(wv_ref[rows[t - 2], :], act.pop(t - 2), (((0,), (0,)), ((), ())),
                                            preferred_element_type=F32)

    @pl.when(j == pl.num_programs(1) - 1)
    def _():
        y = acc_ref[...].T
        out_ref[...] = _layer_norm(DN_ALPHA * h1_ref[...] + y, lg_ref[...], lb_ref[...])


def _peer_dense(h1T, wu, wvT, n_tab, c_tab, r2, e2, h1, lg, lb):
    T = h1T.shape[1]
    tm, eb = TM_PEER, EB_PEER
    col = lambda i, j: (0, i)
    col3 = lambda i, j: (0, 0, i)
    row = lambda i, j: (i, 0)
    return pl.pallas_call(
        functools.partial(_peer_dense_kernel, eb=eb), grid=(T // tm, PEER_EXPERTS // eb),
        in_specs=[pl.BlockSpec((D_MODEL, tm), col),
                  pl.BlockSpec((eb, D_MODEL), lambda i, j: (j, 0)),
                  pl.BlockSpec((eb, D_MODEL), lambda i, j: (j, 0)),
                  pl.BlockSpec((PEER_HEADS, PEER_NKEYS, tm), col3), pl.BlockSpec((PEER_HEADS, PEER_NKEYS, tm), col3),
                  pl.BlockSpec((PEER_HEADS, PEER_NKEYS, tm), col3), pl.BlockSpec((PEER_HEADS, PEER_NKEYS, tm), col3),
                  pl.BlockSpec((tm, D_MODEL), row), _const_spec((1, D_MODEL)), _const_spec((1, D_MODEL))],
        out_specs=pl.BlockSpec((tm, D_MODEL), row),
        out_shape=jax.ShapeDtypeStruct((T, D_MODEL), F32),
        scratch_shapes=[pltpu.VMEM((D_MODEL, tm), F32)],
        compiler_params=_params(("parallel", "arbitrary")), name="peer_dense",
    )(h1T, wu, wvT, n_tab, c_tab, r2, e2, h1, lg, lb)


def _pad_cols(w, lo, total):
    return jnp.pad(w, ((0, 0), (lo, total - lo - w.shape[1])))


def _prepare(S, w_in, b_in, q_norm_g, kv_norm_g, w_uq, w_uk, w_uv, w_o_attn, w_fourier, w_out, b_out, ln1_g, ln1_b,
             peer_w_q, peer_keys, peer_w_u, peer_w_v):
    o_kv, o_kr, o_z, o_g = Q_LORA, Q_LORA + KV_LORA, Q_LORA + KV_LORA + QK_ROPE_DIM, Q_LORA + KV_LORA + QK_ROPE_DIM + F_DIM
    row = lambda v: v.reshape(1, -1).astype(F32)
    w = {}
    w["wq"], w["bq"], w["gq"] = w_in[:, :o_kv].astype(BF16), row(b_in[:o_kv]), row(q_norm_g)
    w["wkv"], w["bkv"], w["gkv"] = w_in[:, o_kv:o_kr].astype(BF16), row(b_in[o_kv:o_kr]), row(kv_norm_g)
    wkr, bkr = w_in[:, o_kr:o_z], b_in[o_kr:o_z].reshape(1, -1)
    swap = lambda m: jnp.concatenate([m[:, ROPE_HALF:], m[:, :ROPE_HALF]], axis=1)
    w["wkr"] = jnp.concatenate([_pad_cols(wkr, QK_NOPE_DIM, HEAD_PAD), _pad_cols(swap(wkr), QK_NOPE_DIM, HEAD_PAD)],
                               axis=1).astype(BF16)
    w["bkr"] = jnp.concatenate([_pad_cols(bkr, QK_NOPE_DIM, HEAD_PAD), _pad_cols(swap(bkr), QK_NOPE_DIM, HEAD_PAD)],
                               axis=1).astype(F32)
    w["wz"], w["bz"] = w_in[:, o_z:o_g].astype(BF16), row(b_in[o_z:o_g])
    w["wg"], w["bg"] = w_in[:, o_g:].astype(BF16), row(b_in[o_g:])
    wq3 = w_uq.reshape(Q_LORA, N_HEADS, QK_DIM).transpose(1, 0, 2)
    rope3 = wq3[:, :, QK_NOPE_DIM:]
    rope3_sw = jnp.concatenate([rope3[:, :, ROPE_HALF:], rope3[:, :, :ROPE_HALF]], axis=2)
    padh = lambda m: jnp.pad(m, ((0, 0), (0, 0), (0, HEAD_PAD - m.shape[2])))
    pair = lambda m: m.reshape(N_HEADS // 2, 2, m.shape[1], HEAD_PAD).transpose(0, 2, 1, 3).reshape(
        N_HEADS // 2, m.shape[1], 2 * HEAD_PAD)
    w["wqu"] = pair(padh(wq3)).astype(BF16)
    w["wqr"] = pair(padh(jnp.concatenate([jnp.zeros_like(wq3[:, :, :QK_NOPE_DIM]), rope3_sw], axis=2))).astype(BF16)
    wk3 = w_uk.reshape(KV_LORA, N_HEADS, QK_NOPE_DIM).transpose(1, 0, 2)
    w["wku"] = pair(padh(wk3)).astype(BF16)
    w["wvu"] = w_uv.astype(BF16)
    pos = jnp.arange(S, dtype=F32)
    inv = 1.0 / (ROPE_THETA ** (jnp.arange(0, QK_ROPE_DIM, 2, dtype=F32) / QK_ROPE_DIM))
    ang = pos[:, None] * inv[None, :]
    cos, sin = jnp.cos(ang), jnp.sin(ang)
    ones, zeros = jnp.ones((S, QK_NOPE_DIM), F32), jnp.zeros((S, QK_NOPE_DIM), F32)
    tail = jnp.zeros((S, HEAD_PAD - QK_DIM), F32)
    scale = QK_DIM ** -0.5 * math.log2(math.e)
    w["cos_q"] = jnp.concatenate([ones, cos, cos, tail], axis=1) * scale
    w["sin_q"] = jnp.concatenate([zeros, -sin, sin, tail], axis=1) * scale
    w["cos_k"] = jnp.concatenate([zeros, cos, cos, tail], axis=1)
    w["sin_k"] = jnp.concatenate([zeros, -sin, sin, tail], axis=1)
    n1 = S // LANES
    idx = lambda n: jnp.arange(n, dtype=jnp.int32)
    ang_c = (2.0 * math.pi / F_GROUP_DIM) * ((idx(F_GROUP_DIM)[:, None] * idx(F_GROUP_DIM)[None, :]) % F_GROUP_DIM).astype(F32)
    w["cs128"] = jnp.concatenate([jnp.cos(ang_c), jnp.sin(ang_c)], axis=1).astype(BF16)
    ang_1 = (2.0 * math.pi / n1) * ((idx(n1)[:, None] * idx(n1)[None, :]) % n1).astype(F32)
    c1, s1 = jnp.cos(ang_1), jnp.sin(ang_1)
    w["m1"] = jnp.concatenate([jnp.concatenate([c1, s1], axis=1), jnp.concatenate([-s1, c1], axis=1)], axis=0).astype(BF16)
    kk = idx(n1)[:, None, None] + n1 * idx(LANES)[None, :, None]
    ang_g = (2.0 * math.pi / S) * ((kk * idx(LANES)[None, None, :]) % S).astype(F32)
    w["gtab"] = jnp.concatenate([jnp.cos(ang_g), jnp.sin(ang_g)], axis=2).astype(BF16)
    w["wo"], w["wf"], w["wout"] = w_o_attn.astype(BF16), w_fourier.astype(BF16), w_out.astype(BF16)
    w["bout"], w["ln1_g"], w["ln1_b"] = row(b_out), row(ln1_g), row(ln1_b)
    w["wqT"] = peer_w_q.astype(BF16).T
    w["keys"] = peer_keys.reshape(2 * PEER_HEADS, PEER_NKEYS, PEER_HALF).astype(BF16)
    w["wu"] = peer_w_u.astype(BF16)
    w["wv"] = peer_w_v.astype(BF16)
    return w


def kernel(x, ln0_g, ln0_b, w_in, b_in, q_norm_g, kv_norm_g, w_uq, w_uk, w_uv, w_o_attn, w_fourier, w_out, b_out, ln1_g,
           ln1_b, peer_w_q, peer_keys, peer_w_u, peer_w_v, ln2_g, ln2_b):
    B, S, D = x.shape
    assert D == D_MODEL and w_in.shape[0] == DEPTH
    T = B * S
    n1 = S // LANES
    assert S % max(TM_FRONT, TQ_ATTN, TM_POST, LANES * SUBLANES) == 0 and T % max(TM_PEER, TM_TOPK) == 0
    row = lambda v: v.reshape(1, -1).astype(F32)
    h = x.reshape(T, D)
    w = _prepare(S, w_in[0], b_in[0], q_norm_g[0], kv_norm_g[0], w_uq[0], w_uk[0], w_uv[0], w_o_attn[0], w_fourier[0],
                 w_out[0], b_out[0], ln1_g[0], ln1_b[0], peer_w_q[0], peer_keys[0], peer_w_u[0], peer_w_v[0])
    h0, qT, k, vT, ur, ui, g = _front(h, B, S, row(ln0_g), row(ln0_b), w)
    oT = _attention(qT, k, vT)
    y = _fft_a(ur.reshape(B, n1, LANES * F_DIM), ui.reshape(B, n1, LANES * F_DIM), w["m1"])
    yf = _fft_b(y.reshape(B, 2, n1, LANES, F_DIM), w["gtab"], 1.0 / math.sqrt(S * F_GROUP_DIM))
    h1, h1T = _post(oT, yf.reshape(T, F_DIM), g, h0, B, S, w)
    n_tab, c_tab, r2, e2 = _peer_topk(h1T, w["wqT"], w["keys"])
    out = _peer_dense(h1T, w["wu"], w["wv"], n_tab, c_tab, r2, e2, h1, row(ln2_g[0]), row(ln2_b[0]))
    return out.reshape(B, S, D)
```

```python
import functools
import math

import jax
import jax.numpy as jnp
from jax import lax
from jax.experimental import pallas as pl
from jax.experimental.pallas import tpu as pltpu

F32 = jnp.float32
BF16 = jnp.bfloat16

D_MODEL = 1024
N_HEADS = 8
QK_NOPE_DIM = 64
QK_ROPE_DIM = 32
ROPE_HALF = QK_ROPE_DIM // 2
QK_DIM = QK_NOPE_DIM + QK_ROPE_DIM
V_DIM = 64
Q_LORA = 256
KV_LORA = 256
ROPE_THETA = 10000.0
F_GROUPS = 4
F_GROUP_DIM = 128
F_DIM = F_GROUPS * F_GROUP_DIM
PEER_HEADS = 8
PEER_NKEYS = 128
PEER_EXPERTS = PEER_NKEYS * PEER_NKEYS
PEER_HALF = 128
PEER_TOPK = 16
DEPTH = 1
DN_ALPHA = (2.0 * DEPTH) ** 0.25
LN_EPS = 1e-5
RMS_EPS = 1e-6
GELU_HALF = 0.5

LANES = 128
SUBLANES = 8
HEAD_PAD = LANES
VMEM_LIMIT_BYTES = 56 * 1024 * 1024

TM_FRONT = 256
TQ_ATTN = 2048
TQ_SUB = 256
TK_ATTN = 512
FFT_A_LANES = 8192
FFT_B_K1 = 16
TM_POST = 512
TM_TOPK = 256
TM_WPREP = 512
TM_PEER = 512
EB_PEER = 2048
PEER_CHAIN = 512

_CAND_GROUPS = ((0, 0, 8), (0, 8, 8), (1, 0, 8), (2, 0, 5), (3, 0, 4), (4, 0, 3), (5, 0, 2), (6, 0, 2), (7, 0, 2))


def _params(sem):
    return pltpu.CompilerParams(dimension_semantics=sem, vmem_limit_bytes=VMEM_LIMIT_BYTES)


def _const_spec(shape):
    nd = len(shape)
    return pl.BlockSpec(shape, lambda *_: (0,) * nd)


def _layer_norm(x, g, b):
    mu = jnp.mean(x, axis=-1, keepdims=True)
    xc = x - mu
    var = jnp.mean(xc * xc, axis=-1, keepdims=True)
    return xc * lax.rsqrt(var + LN_EPS) * g + b


def _rms_norm(x, g):
    return x * lax.rsqrt(jnp.mean(x * x, axis=-1, keepdims=True) + RMS_EPS) * g


def _dot(a, b):
    return jnp.dot(a, b, preferred_element_type=F32)


def _front_kernel(x_ref, g0_ref, b0_ref, wq_ref, bq_ref, gq_ref, wqu_ref, wqr_ref, wkv_ref, bkv_ref, gkv_ref, wku_ref,
                  wvu_ref, wkr_ref, bkr_ref, cq_ref, sq_ref, ck_ref, sk_ref,
                  wz_ref, bz_ref, cs_ref, wg_ref, bg_ref,
                  h0_ref, qT_ref, k_ref, vT_ref, ur_ref, ui_ref, g_ref):
    h0 = _layer_norm(x_ref[...], g0_ref[...], b0_ref[...])
    h0_ref[...] = h0
    hb = h0.astype(BF16)
    cq = _rms_norm(_dot(hb, wq_ref[...]) + bq_ref[...], gq_ref[...]).astype(BF16)
    ckv = _rms_norm(_dot(hb, wkv_ref[...]) + bkv_ref[...], gkv_ref[...]).astype(BF16)
    cos_q, sin_q = cq_ref[...], sq_ref[...]
    kr2 = _dot(hb, wkr_ref[...]) + bkr_ref[...]
    k_rope = kr2[:, :HEAD_PAD] * ck_ref[...] + kr2[:, HEAD_PAD:] * sk_ref[...]
    cos_q2, sin_q2 = jnp.concatenate([cos_q, cos_q], axis=1), jnp.concatenate([sin_q, sin_q], axis=1)
    k_rope2 = jnp.concatenate([k_rope, k_rope], axis=1)
    for hp in range(N_HEADS // 2):
        q_p = _dot(cq, wqu_ref[hp]) * cos_q2 + _dot(cq, wqr_ref[hp]) * sin_q2
        k_p = (_dot(ckv, wku_ref[hp]) + k_rope2).astype(BF16)
        for hh in range(2):
            h = 2 * hp + hh
            lanes = slice(hh * HEAD_PAD, (hh + 1) * HEAD_PAD)
            qT_ref[0, h * HEAD_PAD:(h + 1) * HEAD_PAD, :] = q_p[:, lanes].T.astype(BF16)
            k_ref[0, h] = k_p[:, lanes]
    v_all = _dot(ckv, wvu_ref[...])
    vT_ref[0] = v_all.T.astype(BF16)
    z = (_dot(hb, wz_ref[...]) + bz_ref[...]).astype(BF16)
    cs = cs_ref[...]
    for gi in range(F_GROUPS):
        lo, hi = gi * F_GROUP_DIM, (gi + 1) * F_GROUP_DIM
        pq = _dot(z[:, lo:hi], cs)
        ur_ref[:, lo:hi] = pq[:, :F_GROUP_DIM].astype(BF16)
        ui_ref[:, lo:hi] = (-pq[:, F_GROUP_DIM:]).astype(BF16)
    g_ref[...] = jax.nn.sigmoid(_dot(hb, wg_ref[...]) + bg_ref[...]).astype(BF16)


def _front(x2, B, S, g0, b0, w):
    T = B * S
    tm = TM_FRONT
    nb = S // tm
    tok = lambda b, i: (b * nb + i, 0)
    pos = lambda b, i: (i, 0)
    in_specs = [
        pl.BlockSpec((tm, D_MODEL), tok), _const_spec((1, D_MODEL)), _const_spec((1, D_MODEL)),
        _const_spec((D_MODEL, Q_LORA)), _const_spec((1, Q_LORA)), _const_spec((1, Q_LORA)),
        _const_spec((N_HEADS // 2, Q_LORA, 2 * HEAD_PAD)), _const_spec((N_HEADS // 2, Q_LORA, 2 * HEAD_PAD)),
        _const_spec((D_MODEL, KV_LORA)), _const_spec((1, KV_LORA)), _const_spec((1, KV_LORA)),
        _const_spec((N_HEADS // 2, KV_LORA, 2 * HEAD_PAD)), _const_spec((KV_LORA, N_HEADS * V_DIM)),
        _const_spec((D_MODEL, 2 * HEAD_PAD)), _const_spec((1, 2 * HEAD_PAD)),
        pl.BlockSpec((tm, HEAD_PAD), pos), pl.BlockSpec((tm, HEAD_PAD), pos),
        pl.BlockSpec((tm, HEAD_PAD), pos), pl.BlockSpec((tm, HEAD_PAD), pos),
        _const_spec((D_MODEL, F_DIM)), _const_spec((1, F_DIM)), _const_spec((F_GROUP_DIM, 2 * F_GROUP_DIM)),
        _const_spec((D_MODEL, 2 * D_MODEL)), _const_spec((1, 2 * D_MODEL)),
    ]
    out_specs = [
        pl.BlockSpec((tm, D_MODEL), tok),
        pl.BlockSpec((1, N_HEADS * HEAD_PAD, tm), lambda b, i: (b, 0, i)),
        pl.BlockSpec((1, N_HEADS, tm, HEAD_PAD), lambda b, i: (b, 0, i, 0)),
        pl.BlockSpec((1, N_HEADS * V_DIM, tm), lambda b, i: (b, 0, i)),
        pl.BlockSpec((tm, F_DIM), tok), pl.BlockSpec((tm, F_DIM), tok), pl.BlockSpec((tm, 2 * D_MODEL), tok),
    ]
    out_shape = [
        jax.ShapeDtypeStruct((T, D_MODEL), F32),
        jax.ShapeDtypeStruct((B, N_HEADS * HEAD_PAD, S), BF16),
        jax.ShapeDtypeStruct((B, N_HEADS, S, HEAD_PAD), BF16),
        jax.ShapeDtypeStruct((B, N_HEADS * V_DIM, S), BF16),
        jax.ShapeDtypeStruct((T, F_DIM), BF16), jax.ShapeDtypeStruct((T, F_DIM), BF16),
        jax.ShapeDtypeStruct((T, 2 * D_MODEL), BF16),
    ]
    return pl.pallas_call(
        _front_kernel, grid=(B, nb), in_specs=in_specs, out_specs=out_specs, out_shape=out_shape,
        compiler_params=_params(("parallel", "parallel")), name="front",
    )(x2, g0, b0, w["wq"], w["bq"], w["gq"], w["wqu"], w["wqr"], w["wkv"], w["bkv"], w["gkv"], w["wku"], w["wvu"],
      w["wkr"], w["bkr"], w["cos_q"], w["sin_q"], w["cos_k"], w["sin_k"],
      w["wz"], w["bz"], w["cs128"], w["wg"], w["bg"])


def _col_reduce(x, op, final):
    parts = [x[i:i + SUBLANES] for i in range(0, x.shape[0], SUBLANES)]
    while len(parts) > 1:
        parts = [op(parts[i], parts[i + 1]) for i in range(0, len(parts), 2)]
    return final(parts[0], axis=0, keepdims=True)


def _attn_kernel(qT_ref, k_ref, vT_ref, oT_ref, s_ref, *, tk, n_sub):
    tq = qT_ref.shape[2]
    tsub = tq // n_sub
    n_chunks = k_ref.shape[2] // tk
    subs = range(n_sub)

    def scores(c, slot):
        off = pl.multiple_of(c * tk, tk)
        k_c = k_ref[0, 0, pl.ds(off, tk), :]
        cmax = []
        for u in subs:
            s = _dot(k_c, qT_ref[0, :, u * tsub:(u + 1) * tsub])
            s_ref[slot, u] = s
            cmax.append(_col_reduce(s, jnp.maximum, jnp.max))
        return tuple(cmax)

    def softmax_pv(c, slot, carry, cmax):
        off = pl.multiple_of(c * tk, tk)
        vT_c = vT_ref[0, :, pl.ds(off, tk)]
        m_new = [jnp.maximum(carry[u][0], cmax[u]) for u in subs]
        p = [jnp.exp2(s_ref[slot, u] - m_new[u]) for u in subs]
        alpha = [jnp.exp2(carry[u][0] - m_new[u]) for u in subs]
        l = [alpha[u] * carry[u][1] + _col_reduce(p[u], jnp.add, jnp.sum) for u in subs]
        acc = [alpha[u] * carry[u][2] + _dot(vT_c, p[u].astype(BF16)) for u in subs]
        return tuple((m_new[u], l[u], acc[u]) for u in subs)

    def body(i, carry):
        state, cmax0 = carry
        c0 = 2 * i
        cmax1 = scores(c0 + 1, 1)
        state = softmax_pv(c0, 0, state, cmax0)
        cmax0 = scores(c0 + 2, 0)
        return softmax_pv(c0 + 1, 1, state, cmax1), cmax0

    cmax_first = scores(0, 0)
    init = tuple((jnp.full((1, tsub), -jnp.inf, F32), jnp.zeros((1, tsub), F32), jnp.zeros((V_DIM, tsub), F32))
                 for _ in range(n_sub))
    state, cmax0 = lax.fori_loop(0, n_chunks // 2 - 1, body, (init, cmax_first))
    cmax1 = scores(n_chunks - 1, 1)
    state = softmax_pv(n_chunks - 2, 0, state, cmax0)
    fin = softmax_pv(n_chunks - 1, 1, state, cmax1)
    for u in range(n_sub):
        _, l, acc = fin[u]
        oT_ref[0, :, u * tsub:(u + 1) * tsub] = acc * (1.0 / l)


def _attention(qT, k, vT):
    B, _, S = qT.shape
    tq = TQ_ATTN
    return pl.pallas_call(
        functools.partial(_attn_kernel, tk=min(TK_ATTN, S), n_sub=TQ_ATTN // TQ_SUB),
        grid=(B, N_HEADS, S // tq),
        in_specs=[
            pl.BlockSpec((1, HEAD_PAD, tq), lambda b, h, i: (b, h, i)),
            pl.BlockSpec((1, 1, S, HEAD_PAD), lambda b, h, i: (b, h, 0, 0)),
            pl.BlockSpec((1, V_DIM, S), lambda b, h, i: (b, h, 0)),
        ],
        out_specs=pl.BlockSpec((1, V_DIM, tq), lambda b, h, i: (b, h, i)),
        out_shape=jax.ShapeDtypeStruct((B, N_HEADS * V_DIM, S), F32),
        scratch_shapes=[pltpu.VMEM((2, TQ_ATTN // TQ_SUB, min(TK_ATTN, S), TQ_SUB), F32)],
        compiler_params=_params(("parallel", "parallel", "parallel")),
        name="attention",
    )(qT, k, vT)


def _fft_a_kernel(ur_ref, ui_ref, m1_ref, y_ref):
    u = jnp.concatenate([ur_ref[0], ui_ref[0]], axis=0)
    y_ref[0] = _dot(m1_ref[...], u).astype(BF16)


def _fft_a(ur3, ui3, m1):
    B, n1, W = ur3.shape
    L = min(FFT_A_LANES, W)
    blk = lambda b, j: (b, 0, j)
    return pl.pallas_call(
        _fft_a_kernel, grid=(B, W // L),
        in_specs=[pl.BlockSpec((1, n1, L), blk), pl.BlockSpec((1, n1, L), blk), _const_spec((2 * n1, 2 * n1))],
        out_specs=pl.BlockSpec((1, 2 * n1, L), blk),
        out_shape=jax.ShapeDtypeStruct((B, 2 * n1, W), BF16),
        compiler_params=_params(("parallel", "parallel")), name="fft_a",
    )(ur3, ui3, m1)


def _fft_b_kernel(y_ref, g_ref, o_ref, *, nk, scale):
    for j in range(nk):
        ycat = jnp.concatenate([y_ref[0, 0, j], y_ref[0, 1, j]], axis=0)
        o_ref[0, :, j * F_DIM:(j + 1) * F_DIM] = (_dot(g_ref[j], ycat) * scale).astype(BF16)


def _fft_b(y5, gtab, scale):
    B, _, n1, _, _ = y5.shape
    nk = min(FFT_B_K1, n1)
    return pl.pallas_call(
        functools.partial(_fft_b_kernel, nk=nk, scale=scale), grid=(B, n1 // nk),
        in_specs=[pl.BlockSpec((1, 2, nk, LANES, F_DIM), lambda b, j: (b, 0, j, 0, 0)),
                  pl.BlockSpec((nk, LANES, 2 * LANES), lambda b, j: (j, 0, 0))],
        out_specs=pl.BlockSpec((1, LANES, nk * F_DIM), lambda b, j: (b, 0, j)),
        out_shape=jax.ShapeDtypeStruct((B, LANES, n1 * F_DIM), BF16),
        compiler_params=_params(("parallel", "parallel")), name="fft_b",
    )(y5, gtab)


def _post_kernel(oT_ref, yf_ref, g_ref, h0_ref, wo_ref, wf_ref, wout_ref, bout_ref, lg_ref, lb_ref, h1_ref, h1T_ref):
    o = oT_ref[0].T.astype(BF16)
    y_a = _dot(o, wo_ref[...])
    y_f = _dot(yf_ref[...], wf_ref[...])
    g = g_ref[...].astype(F32)
    m = (g[:, :D_MODEL] * y_a + g[:, D_MODEL:] * y_f).astype(BF16)
    mix = _dot(m, wout_ref[...]) + bout_ref[...]
    h1 = _layer_norm(DN_ALPHA * h0_ref[...] + mix, lg_ref[...], lb_ref[...])
    h1_ref[...] = h1
    h1T_ref[...] = h1.T.astype(BF16)


def _post(oT, yf, g, h0, B, S, w):
    T = B * S
    tm = TM_POST
    nb = S // tm
    tok = lambda b, i: (b * nb + i, 0)
    return pl.pallas_call(
        _post_kernel, grid=(B, nb),
        in_specs=[pl.BlockSpec((1, N_HEADS * V_DIM, tm), lambda b, i: (b, 0, i)),
                  pl.BlockSpec((tm, F_DIM), tok), pl.BlockSpec((tm, 2 * D_MODEL), tok), pl.BlockSpec((tm, D_MODEL), tok),
                  _const_spec((N_HEADS * V_DIM, D_MODEL)), _const_spec((F_DIM, D_MODEL)), _const_spec((D_MODEL, D_MODEL)),
                  _const_spec((1, D_MODEL)), _const_spec((1, D_MODEL)), _const_spec((1, D_MODEL))],
        out_specs=[pl.BlockSpec((tm, D_MODEL), tok), pl.BlockSpec((D_MODEL, tm), lambda b, i: (0, b * nb + i))],
        out_shape=[jax.ShapeDtypeStruct((T, D_MODEL), F32), jax.ShapeDtypeStruct((D_MODEL, T), BF16)],
        compiler_params=_params(("parallel", "parallel")), name="post",
    )(oT, yf, g, h0, w["wo"], w["wf"], w["wout"], w["bout"], w["ln1_g"], w["ln1_b"])


def _extract_top16(s):
    row = lax.broadcasted_iota(jnp.int32, s.shape, 0).astype(F32)
    slot = lax.broadcasted_iota(jnp.int32, (PEER_TOPK, s.shape[1]), 0)
    rank = jnp.full(s.shape, float(PEER_TOPK), F32)
    vals = jnp.zeros((PEER_TOPK, s.shape[1]), F32)
    for r in range(PEER_TOPK):
        m = jnp.max(s, axis=0, keepdims=True)
        idx = jnp.min(jnp.where(s == m, row, float(PEER_NKEYS)), axis=0, keepdims=True)
        hit = row == idx
        s = jnp.where(hit, -jnp.inf, s)
        rank = jnp.where(hit, float(r), rank)
        vals = jnp.where(slot == r, m, vals)
    return vals, rank


def _cand_rows():
    groups = [[(i, j0 + r) if r < valid else None for r in range(SUBLANES)] for (i, j0, valid) in _CAND_GROUPS]
    groups.append([(SUBLANES + r, 0) for r in range(SUBLANES)])
    return groups


def _static_beats(cp, c):
    if cp == c:
        return 0
    if cp[0] <= c[0] and cp[1] <= c[1]:
        return 1
    if cp[0] >= c[0] and cp[1] >= c[1]:
        return 0
    return None


def _static_counts():
    rows = _cand_rows()
    out = [[0.0] * SUBLANES for _ in rows]
    for cg in rows:
        for cp in cg:
            if cp is None:
                continue
            for g, tgt in enumerate(rows):
                res = [_static_beats(cp, c) if c is not None else 0 for c in tgt]
                if all(v is not None for v in res):
                    for r in range(SUBLANES):
                        out[g][r] += float(res[r])
    return out


def _select_pairs(a_rep, b_lo, b_hi, a_hi, static_counts):
    L = b_lo.shape[1]
    sub = lax.broadcasted_iota(jnp.int32, (SUBLANES, L), 0)
    rows = _cand_rows()
    groups = []
    for (i, j0, valid) in _CAND_GROUPS:
        v = a_rep[i] + (b_lo if j0 == 0 else b_hi)
        if valid < SUBLANES:
            v = jnp.where(sub < valid, v, -jnp.inf)
        groups.append(v)
    groups.append(a_hi + jnp.broadcast_to(b_lo[0:1, :], (SUBLANES, L)))
    n_groups = len(groups)
    counts = [static_counts[g] for g in range(n_groups)]
    for gp in range(n_groups):
        for rp in range(SUBLANES):
            cp = rows[gp][rp]
            if cp is None:
                continue
            vb = jnp.broadcast_to(groups[gp][rp:rp + 1, :], (SUBLANES, L))
            for g in range(n_groups):
                if all(c is None or _static_beats(cp, c) is not None for c in rows[g]):
                    continue
                if g < gp:
                    beats = jnp.where(vb > groups[g], 1.0, 0.0)
                elif g > gp:
                    beats = jnp.where(vb >= groups[g], 1.0, 0.0)
                else:
                    beats = jnp.where(sub > rp, jnp.where(vb >= groups[g], 1.0, 0.0), jnp.where(vb > groups[g], 1.0, 0.0))
                counts[g] = counts[g] + beats
    top = groups[0][0:1, :]
    valid_rows = [g[2] for g in _CAND_GROUPS] + [SUBLANES]
    sel = []
    z = jnp.zeros((1, L), F32)
    for g in range(n_groups):
        s_g = jnp.where(counts[g] < float(PEER_TOPK), 1.0, 0.0)
        if valid_rows[g] < SUBLANES:
            s_g = jnp.where(sub < valid_rows[g], s_g, 0.0)
        sel.append(s_g)
        z = z + jnp.sum(s_g * jnp.exp(groups[g] - top), axis=0, keepdims=True)
    n = [jnp.sum(sel[0] + sel[1], axis=0, keepdims=True)]
    for g in range(2, n_groups - 1):
        n.append(jnp.sum(sel[g], axis=0, keepdims=True))
    for r in range(SUBLANES):
        n.append(sel[n_groups - 1][r:r + 1, :])
    return n, z


def _rows_from_rep(rep, lo):
    sub = lax.broadcasted_iota(jnp.int32, rep[0].shape, 0)
    out = rep[lo]
    for r in range(1, SUBLANES):
        out = jnp.where(sub == r, rep[lo + r], out)
    return out


def _oddeven_merge_pairs(n):
    pairs = []
    t = n.bit_length() - 1
    for pi in range(t):
        p = 1 << pi
        for ki in range(pi, -1, -1):
            k = 1 << ki
            for j in range(k % p, n - k, 2 * k):
                for i in range(min(k, n - j - k)):
                    if (i + j) // (2 * p) == (i + j + k) // (2 * p):
                        pairs.append((i + j, i + j + k))
    return pairs


_SORT16 = _oddeven_merge_pairs(PEER_TOPK)


def _top16_values(s):
    w = [s[g * SUBLANES:(g + 1) * SUBLANES] for g in range(PEER_NKEYS // SUBLANES)]
    for (i, j) in _SORT16:
        w[i], w[j] = jnp.maximum(w[i], w[j]), jnp.minimum(w[i], w[j])
    for shift in (4, 2, 1):
        other = [pltpu.roll(x, shift, axis=0) for x in w]
        w = [jnp.maximum(w[i], other[PEER_TOPK - 1 - i]) for i in range(PEER_TOPK)]
        d = PEER_TOPK // 2
        while d >= 1:
            for i in range(PEER_TOPK):
                if i & d == 0:
                    w[i], w[i + d] = jnp.maximum(w[i], w[i + d]), jnp.minimum(w[i], w[i + d])
            d //= 2
    return w


def _tie_flags(s, w):
    flag = jnp.zeros_like(w[0])
    for r in range(PEER_TOPK - 1):
        flag = flag + jnp.where(w[r] == w[r + 1], 1.0, 0.0)
    parts = [jnp.where(s[g * SUBLANES:(g + 1) * SUBLANES] >= w[PEER_TOPK - 1], 1.0, 0.0)
             for g in range(PEER_NKEYS // SUBLANES)]
    while len(parts) > 1:
        parts = [parts[i] + parts[i + 1] for i in range(0, len(parts), 2)]
    count = jnp.sum(parts[0], axis=0, keepdims=True)
    return flag + (count - float(PEER_TOPK))


def _peer_topk_kernel(h1T_ref, wqT_ref, keys_ref, sc_ref, n_ref, c_ref, r2_ref, e2_ref, qp_ref):
    tm = h1T_ref.shape[1]
    qp_ref[...] = _dot(wqT_ref[...], h1T_ref[...]).astype(BF16)

    def scores(h, lo):
        r0, r1 = 2 * h * PEER_HALF, (2 * h + 1) * PEER_HALF
        if not isinstance(h, int):
            r0, r1 = pl.multiple_of(r0, PEER_HALF), pl.multiple_of(r1, PEER_HALF)
        s1 = _dot(keys_ref[2 * h], qp_ref[pl.ds(r0, PEER_HALF), pl.ds(lo, LANES)])
        s2 = _dot(keys_ref[2 * h + 1], qp_ref[pl.ds(r1, PEER_HALF), pl.ds(lo, LANES)])
        return s1, s2

    def store(h, lo, n_a, c_a, rank2, e2):
        n_ref[h, :, pl.ds(lo, LANES)] = n_a
        c_ref[h, :, pl.ds(lo, LANES)] = c_a
        r2_ref[h, :, pl.ds(lo, LANES)] = rank2.astype(BF16)
        e2_ref[h, :, pl.ds(lo, LANES)] = e2.astype(BF16)

    def fast_head(h, lo):
        s1, s2 = scores(h, lo)
        w1, w2 = _top16_values(s1), _top16_values(s2)
        n, z = _select_pairs(w1, _rows_from_rep(w2, 0), _rows_from_rep(w2, SUBLANES), _rows_from_rep(w1, SUBLANES),
                             sc_ref)
        n_parts, r_parts = [], []
        for g in range(PEER_NKEYS // SUBLANES):
            s1_g, s2_g = s1[g * SUBLANES:(g + 1) * SUBLANES], s2[g * SUBLANES:(g + 1) * SUBLANES]
            n_g = jnp.zeros_like(s1_g)
            r_g = jnp.full_like(s2_g, float(PEER_TOPK))
            for r in range(PEER_TOPK):
                n_g = jnp.where(s1_g == w1[r], n[r], n_g)
                r_g = jnp.where(s2_g == w2[r], float(r), r_g)
            n_parts.append(n_g)
            r_parts.append(r_g)
        c_a = jnp.exp(s1 - w1[0][0:1, :]) * (GELU_HALF / z)
        e2 = jnp.exp(s2 - w2[0][0:1, :])
        store(h, lo, jnp.concatenate(n_parts, axis=0), c_a, jnp.concatenate(r_parts, axis=0), e2)
        return _tie_flags(s1, w1) + _tie_flags(s2, w2)

    def exact_head(h, lo):
        s1, s2 = scores(h, lo)
        a_vals, rank1 = _extract_top16(s1)
        b_vals, rank2 = _extract_top16(s2)
        a_rep = [jnp.broadcast_to(a_vals[i:i + 1, :], (SUBLANES, LANES)) for i in range(PEER_TOPK)]
        n, z = _select_pairs(a_rep, b_vals[:SUBLANES], b_vals[SUBLANES:], a_vals[SUBLANES:], sc_ref)
        n_a = jnp.zeros_like(s1)
        for i in range(PEER_TOPK):
            n_a = n_a + jnp.where(rank1 == float(i), n[i], 0.0)
        c_a = jnp.exp(s1 - a_vals[0:1, :]) * (GELU_HALF / z)
        e2 = jnp.exp(s2 - b_vals[0:1, :])
        store(h, lo, n_a, c_a, rank2, e2)

    def chunk(ci, _):
        lo = pl.multiple_of(ci * LANES, LANES)
        flags = jnp.zeros((SUBLANES, LANES), F32)
        for h in range(PEER_HEADS):
            flags = flags + fast_head(h, lo)

        @pl.when(jnp.max(flags) > 0.0)
        def _():
            def per_head(h, carry):
                exact_head(h, lo)
                return carry
            lax.fori_loop(0, PEER_HEADS, per_head, 0)
        return 0

    lax.fori_loop(0, tm // LANES, chunk, 0)


def _peer_topk(h1T, wqT, keys):
    T = h1T.shape[1]
    tm = TM_TOPK
    nq = wqT.shape[0]
    col = lambda i: (0, i)
    col3 = lambda i: (0, 0, i)
    static = jnp.broadcast_to(jnp.asarray(_static_counts(), F32)[:, :, None], (len(_CAND_GROUPS) + 1, SUBLANES, LANES))
    return pl.pallas_call(
        _peer_topk_kernel, grid=(T // tm,),
        in_specs=[pl.BlockSpec((D_MODEL, tm), col), _const_spec((nq, D_MODEL)),
                  _const_spec((2 * PEER_HEADS, PEER_NKEYS, PEER_HALF)), _const_spec(static.shape)],
        out_specs=[pl.BlockSpec((PEER_HEADS, PEER_NKEYS, tm), col3), pl.BlockSpec((PEER_HEADS, PEER_NKEYS, tm), col3),
                   pl.BlockSpec((PEER_HEADS, PEER_NKEYS, tm), col3), pl.BlockSpec((PEER_HEADS, PEER_NKEYS, tm), col3)],
        out_shape=[jax.ShapeDtypeStruct((PEER_HEADS, PEER_NKEYS, T), F32),
                   jax.ShapeDtypeStruct((PEER_HEADS, PEER_NKEYS, T), F32),
                   jax.ShapeDtypeStruct((PEER_HEADS, PEER_NKEYS, T), BF16),
                   jax.ShapeDtypeStruct((PEER_HEADS, PEER_NKEYS, T), BF16)],
        scratch_shapes=[pltpu.VMEM((nq, tm), BF16)],
        compiler_params=_params(("parallel",)), name="peer_topk",
    )(h1T, wqT, keys, static)


def _peer_dense_kernel(xT_ref, wu_ref, wvT_ref, n_ref, c_ref, r2_ref, e2_ref, h1_ref, lg_ref, lb_ref, out_ref,
                       acc_ref, *, eb):
    j = pl.program_id(1)
    tm = xT_ref.shape[1]

    @pl.when(j == 0)
    def _():
        acc_ref[...] = jnp.zeros_like(acc_ref)

    n_chains = eb // PEER_CHAIN
    rows = [slice(ch * PEER_CHAIN, (ch + 1) * PEER_CHAIN) for ch in range(n_chains)]

    def first_matmul(ch):
        return _dot(wu_ref[rows[ch], :], xT_ref[...])

    def gate_gelu(ch, hpre):
        acts = []
        for aa in range(PEER_CHAIN // PEER_NKEYS):
            a_key = (j * n_chains + ch) * (PEER_CHAIN // PEER_NKEYS) + aa
            gate = jnp.zeros((PEER_NKEYS, tm), BF16)
            for h in range(PEER_HEADS):
                n_b = jnp.broadcast_to(n_ref[h, pl.ds(a_key, 1), :], (PEER_NKEYS, tm)).astype(BF16)
                c_b = jnp.broadcast_to(c_ref[h, pl.ds(a_key, 1), :], (PEER_NKEYS, tm)).astype(BF16)
                gate = gate + jnp.where(r2_ref[h] < n_b, e2_ref[h], jnp.zeros((), BF16)) * c_b
            hp = hpre[aa * PEER_NKEYS:(aa + 1) * PEER_NKEYS, :]
            act = hp * (1.0 + lax.erf(hp * (1.0 / math.sqrt(2.0))))
            acts.append(act.astype(BF16) * gate)
        return jnp.concatenate(acts, axis=0)

    hpre, act = {}, {}
    for t in range(n_chains + 2):
        if t < n_chains:
            hpre[t] = first_matmul(t)
        if 0 <= t - 1 < n_chains:
            act[t - 1] = gate_gelu(t - 1, hpre.pop(t - 1))
        if 0 <= t - 2 < n_chains:
            acc_ref[...] += _dot(wvT_ref[:, rows[t - 2]], act.pop(t - 2))

    @pl.when(j == pl.num_programs(1) - 1)
    def _():
        y = acc_ref[...].T
        out_ref[...] = _layer_norm(DN_ALPHA * h1_ref[...] + y, lg_ref[...], lb_ref[...])


def _peer_dense(h1T, wu, wvT, n_tab, c_tab, r2, e2, h1, lg, lb):
    T = h1T.shape[1]
    tm, eb = TM_PEER, EB_PEER
    col = lambda i, j: (0, i)
    col3 = lambda i, j: (0, 0, i)
    row = lambda i, j: (i, 0)
    return pl.pallas_call(
        functools.partial(_peer_dense_kernel, eb=eb), grid=(T // tm, PEER_EXPERTS // eb),
        in_specs=[pl.BlockSpec((D_MODEL, tm), col),
                  pl.BlockSpec((eb, D_MODEL), lambda i, j: (j, 0)),
                  pl.BlockSpec((D_MODEL, eb), lambda i, j: (0, j)),
                  pl.BlockSpec((PEER_HEADS, PEER_NKEYS, tm), col3), pl.BlockSpec((PEER_HEADS, PEER_NKEYS, tm), col3),
                  pl.BlockSpec((PEER_HEADS, PEER_NKEYS, tm), col3), pl.BlockSpec((PEER_HEADS, PEER_NKEYS, tm), col3),
                  pl.BlockSpec((tm, D_MODEL), row), _const_spec((1, D_MODEL)), _const_spec((1, D_MODEL))],
        out_specs=pl.BlockSpec((tm, D_MODEL), row),
        out_shape=jax.ShapeDtypeStruct((T, D_MODEL), F32),
        scratch_shapes=[pltpu.VMEM((D_MODEL, tm), F32)],
        compiler_params=_params(("parallel", "arbitrary")), name="peer_dense",
    )(h1T, wu, wvT, n_tab, c_tab, r2, e2, h1, lg, lb)


def _transpose_cast_kernel(w_ref, o_ref):
    o_ref[...] = w_ref[...].T.astype(BF16)


def _transpose_cast(w):
    rows, d = w.shape
    tr = TM_WPREP
    return pl.pallas_call(
        _transpose_cast_kernel, grid=(rows // tr,),
        in_specs=[pl.BlockSpec((tr, d), lambda i: (i, 0))],
        out_specs=pl.BlockSpec((d, tr), lambda i: (0, i)),
        out_shape=jax.ShapeDtypeStruct((d, rows), BF16),
        compiler_params=_params(("parallel",)), name="wv_prep",
    )(w)


def _pad_cols(w, lo, total):
    return jnp.pad(w, ((0, 0), (lo, total - lo - w.shape[1])))


def _prepare(S, w_in, b_in, q_norm_g, kv_norm_g, w_uq, w_uk, w_uv, w_o_attn, w_fourier, w_out, b_out, ln1_g, ln1_b,
             peer_w_q, peer_keys, peer_w_u, peer_w_v):
    o_kv, o_kr, o_z, o_g = Q_LORA, Q_LORA + KV_LORA, Q_LORA + KV_LORA + QK_ROPE_DIM, Q_LORA + KV_LORA + QK_ROPE_DIM + F_DIM
    row = lambda v: v.reshape(1, -1).astype(F32)
    w = {}
    w["wq"], w["bq"], w["gq"] = w_in[:, :o_kv].astype(BF16), row(b_in[:o_kv]), row(q_norm_g)
    w["wkv"], w["bkv"], w["gkv"] = w_in[:, o_kv:o_kr].astype(BF16), row(b_in[o_kv:o_kr]), row(kv_norm_g)
    wkr, bkr = w_in[:, o_kr:o_z], b_in[o_kr:o_z].reshape(1, -1)
    swap = lambda m: jnp.concatenate([m[:, ROPE_HALF:], m[:, :ROPE_HALF]], axis=1)
    w["wkr"] = jnp.concatenate([_pad_cols(wkr, QK_NOPE_DIM, HEAD_PAD), _pad_cols(swap(wkr), QK_NOPE_DIM, HEAD_PAD)],
                               axis=1).astype(BF16)
    w["bkr"] = jnp.concatenate([_pad_cols(bkr, QK_NOPE_DIM, HEAD_PAD), _pad_cols(swap(bkr), QK_NOPE_DIM, HEAD_PAD)],
                               axis=1).astype(F32)
    w["wz"], w["bz"] = w_in[:, o_z:o_g].astype(BF16), row(b_in[o_z:o_g])
    w["wg"], w["bg"] = w_in[:, o_g:].astype(BF16), row(b_in[o_g:])
    wq3 = w_uq.reshape(Q_LORA, N_HEADS, QK_DIM).transpose(1, 0, 2)
    rope3 = wq3[:, :, QK_NOPE_DIM:]
    rope3_sw = jnp.concatenate([rope3[:, :, ROPE_HALF:], rope3[:, :, :ROPE_HALF]], axis=2)
    padh = lambda m: jnp.pad(m, ((0, 0), (0, 0), (0, HEAD_PAD - m.shape[2])))
    pair = lambda m: m.reshape(N_HEADS // 2, 2, m.shape[1], HEAD_PAD).transpose(0, 2, 1, 3).reshape(
        N_HEADS // 2, m.shape[1], 2 * HEAD_PAD)
    w["wqu"] = pair(padh(wq3)).astype(BF16)
    w["wqr"] = pair(padh(jnp.concatenate([jnp.zeros_like(wq3[:, :, :QK_NOPE_DIM]), rope3_sw], axis=2))).astype(BF16)
    wk3 = w_uk.reshape(KV_LORA, N_HEADS, QK_NOPE_DIM).transpose(1, 0, 2)
    w["wku"] = pair(padh(wk3)).astype(BF16)
    w["wvu"] = w_uv.astype(BF16)
    pos = jnp.arange(S, dtype=F32)
    inv = 1.0 / (ROPE_THETA ** (jnp.arange(0, QK_ROPE_DIM, 2, dtype=F32) / QK_ROPE_DIM))
    ang = pos[:, None] * inv[None, :]
    cos, sin = jnp.cos(ang), jnp.sin(ang)
    ones, zeros = jnp.ones((S, QK_NOPE_DIM), F32), jnp.zeros((S, QK_NOPE_DIM), F32)
    tail = jnp.zeros((S, HEAD_PAD - QK_DIM), F32)
    scale = QK_DIM ** -0.5 * math.log2(math.e)
    w["cos_q"] = jnp.concatenate([ones, cos, cos, tail], axis=1) * scale
    w["sin_q"] = jnp.concatenate([zeros, -sin, sin, tail], axis=1) * scale
    w["cos_k"] = jnp.concatenate([zeros, cos, cos, tail], axis=1)
    w["sin_k"] = jnp.concatenate([zeros, -sin, sin, tail], axis=1)
    n1 = S // LANES
    idx = lambda n: jnp.arange(n, dtype=jnp.int32)
    ang_c = (2.0 * math.pi / F_GROUP_DIM) * ((idx(F_GROUP_DIM)[:, None] * idx(F_GROUP_DIM)[None, :]) % F_GROUP_DIM).astype(F32)
    w["cs128"] = jnp.concatenate([jnp.cos(ang_c), jnp.sin(ang_c)], axis=1).astype(BF16)
    ang_1 = (2.0 * math.pi / n1) * ((idx(n1)[:, None] * idx(n1)[None, :]) % n1).astype(F32)
    c1, s1 = jnp.cos(ang_1), jnp.sin(ang_1)
    w["m1"] = jnp.concatenate([jnp.concatenate([c1, s1], axis=1), jnp.concatenate([-s1, c1], axis=1)], axis=0).astype(BF16)
    kk = idx(n1)[:, None, None] + n1 * idx(LANES)[None, :, None]
    ang_g = (2.0 * math.pi / S) * ((kk * idx(LANES)[None, None, :]) % S).astype(F32)
    w["gtab"] = jnp.concatenate([jnp.cos(ang_g), jnp.sin(ang_g)], axis=2).astype(BF16)
    w["wo"], w["wf"], w["wout"] = w_o_attn.astype(BF16), w_fourier.astype(BF16), w_out.astype(BF16)
    w["bout"], w["ln1_g"], w["ln1_b"] = row(b_out), row(ln1_g), row(ln1_b)
    w["wqT"] = peer_w_q.astype(BF16).T
    w["keys"] = peer_keys.reshape(2 * PEER_HEADS, PEER_NKEYS, PEER_HALF).astype(BF16)
    w["wu"] = peer_w_u.astype(BF16)
    w["wvT"] = _transpose_cast(peer_w_v)
    return w


def kernel(x, ln0_g, ln0_b, w_in, b_in, q_norm_g, kv_norm_g, w_uq, w_uk, w_uv, w_o_attn, w_fourier, w_out, b_out, ln1_g,
           ln1_b, peer_w_q, peer_keys, peer_w_u, peer_w_v, ln2_g, ln2_b):
    B, S, D = x.shape
    assert D == D_MODEL and w_in.shape[0] == DEPTH
    T = B * S
    n1 = S // LANES
    assert S % max(TM_FRONT, TQ_ATTN, TM_POST, LANES * SUBLANES) == 0 and T % max(TM_PEER, TM_TOPK) == 0
    row = lambda v: v.reshape(1, -1).astype(F32)
    h = x.reshape(T, D)
    w = _prepare(S, w_in[0], b_in[0], q_norm_g[0], kv_norm_g[0], w_uq[0], w_uk[0], w_uv[0], w_o_attn[0], w_fourier[0],
                 w_out[0], b_out[0], ln1_g[0], ln1_b[0], peer_w_q[0], peer_keys[0], peer_w_u[0], peer_w_v[0])
    h0, qT, k, vT, ur, ui, g = _front(h, B, S, row(ln0_g), row(ln0_b), w)
    oT = _attention(qT, k, vT)
    y = _fft_a(ur.reshape(B, n1, LANES * F_DIM), ui.reshape(B, n1, LANES * F_DIM), w["m1"])
    yf = _fft_b(y.reshape(B, 2, n1, LANES, F_DIM), w["gtab"], 1.0 / math.sqrt(S * F_GROUP_DIM))
    h1, h1T = _post(oT, yf.reshape(T, F_DIM), g, h0, B, S, w)
    n_tab, c_tab, r2, e2 = _peer_topk(h1T, w["wqT"], w["keys"])
    out = _peer_dense(h1T, w["wu"], w["wvT"], n_tab, c_tab, r2, e2, h1, row(ln2_g[0]), row(ln2_b[0]))
    return out.reshape(B, S, D)
```

```python
import functools
import math

import jax
import jax.numpy as jnp
from jax import lax
from jax.experimental import pallas as pl
from jax.experimental.pallas import tpu as pltpu

F32 = jnp.float32
BF16 = jnp.bfloat16

D_MODEL = 1024
N_HEADS = 8
QK_NOPE_DIM = 64
QK_ROPE_DIM = 32
ROPE_HALF = QK_ROPE_DIM // 2
QK_DIM = QK_NOPE_DIM + QK_ROPE_DIM
V_DIM = 64
Q_LORA = 256
KV_LORA = 256
ROPE_THETA = 10000.0
F_GROUPS = 4
F_GROUP_DIM = 128
F_DIM = F_GROUPS * F_GROUP_DIM
PEER_HEADS = 8
PEER_NKEYS = 128
PEER_EXPERTS = PEER_NKEYS * PEER_NKEYS
PEER_HALF = 128
PEER_TOPK = 16
DEPTH = 1
DN_ALPHA = (2.0 * DEPTH) ** 0.25
LN_EPS = 1e-5
RMS_EPS = 1e-6
GELU_HALF = 0.5

LANES = 128
SUBLANES = 8
HEAD_PAD = LANES
VMEM_LIMIT_BYTES = 56 * 1024 * 1024

TM_FRONT = 256
TQ_ATTN = 2048
TQ_SUB = 256
TK_ATTN = 512
FFT_A_LANES = 8192
FFT_B_K1 = 16
TM_POST = 512
TM_TOPK = 256
TM_WPREP = 2048
TM_PEER = 512
EB_PEER = 2048
PEER_CHAIN = 512

_CAND_GROUPS = ((0, 0, 8), (0, 8, 8), (1, 0, 8), (2, 0, 5), (3, 0, 4), (4, 0, 3), (5, 0, 2), (6, 0, 2), (7, 0, 2))


def _params(sem):
    return pltpu.CompilerParams(dimension_semantics=sem, vmem_limit_bytes=VMEM_LIMIT_BYTES)


def _const_spec(shape):
    nd = len(shape)
    return pl.BlockSpec(shape, lambda *_: (0,) * nd)


def _layer_norm(x, g, b):
    mu = jnp.mean(x, axis=-1, keepdims=True)
    xc = x - mu
    var = jnp.mean(xc * xc, axis=-1, keepdims=True)
    return xc * lax.rsqrt(var + LN_EPS) * g + b


def _rms_norm(x, g):
    return x * lax.rsqrt(jnp.mean(x * x, axis=-1, keepdims=True) + RMS_EPS) * g


def _dot(a, b):
    return jnp.dot(a, b, preferred_element_type=F32)


def _front_kernel(x_ref, g0_ref, b0_ref, wq_ref, bq_ref, gq_ref, wqu_ref, wqr_ref, wkv_ref, bkv_ref, gkv_ref, wku_ref,
                  wvu_ref, wkr_ref, bkr_ref, cq_ref, sq_ref, ck_ref, sk_ref,
                  wz_ref, bz_ref, cs_ref, wg_ref, bg_ref,
                  h0_ref, qT_ref, k_ref, vT_ref, ur_ref, ui_ref, g_ref):
    h0 = _layer_norm(x_ref[...], g0_ref[...], b0_ref[...])
    h0_ref[...] = h0
    hb = h0.astype(BF16)
    cq = _rms_norm(_dot(hb, wq_ref[...]) + bq_ref[...], gq_ref[...]).astype(BF16)
    ckv = _rms_norm(_dot(hb, wkv_ref[...]) + bkv_ref[...], gkv_ref[...]).astype(BF16)
    cos_q, sin_q = cq_ref[...], sq_ref[...]
    kr2 = _dot(hb, wkr_ref[...]) + bkr_ref[...]
    k_rope = kr2[:, :HEAD_PAD] * ck_ref[...] + kr2[:, HEAD_PAD:] * sk_ref[...]
    cos_q2, sin_q2 = jnp.concatenate([cos_q, cos_q], axis=1), jnp.concatenate([sin_q, sin_q], axis=1)
    k_rope2 = jnp.concatenate([k_rope, k_rope], axis=1)
    for hp in range(N_HEADS // 2):
        q_p = _dot(cq, wqu_ref[hp]) * cos_q2 + _dot(cq, wqr_ref[hp]) * sin_q2
        k_p = (_dot(ckv, wku_ref[hp]) + k_rope2).astype(BF16)
        for hh in range(2):
            h = 2 * hp + hh
            lanes = slice(hh * HEAD_PAD, (hh + 1) * HEAD_PAD)
            qT_ref[0, h * HEAD_PAD:(h + 1) * HEAD_PAD, :] = q_p[:, lanes].T.astype(BF16)
            k_ref[0, h] = k_p[:, lanes]
    v_all = _dot(ckv, wvu_ref[...])
    vT_ref[0] = v_all.T.astype(BF16)
    z = (_dot(hb, wz_ref[...]) + bz_ref[...]).astype(BF16)
    cs = cs_ref[...]
    for gi in range(F_GROUPS):
        lo, hi = gi * F_GROUP_DIM, (gi + 1) * F_GROUP_DIM
        pq = _dot(z[:, lo:hi], cs)
        ur_ref[:, lo:hi] = pq[:, :F_GROUP_DIM].astype(BF16)
        ui_ref[:, lo:hi] = (-pq[:, F_GROUP_DIM:]).astype(BF16)
    g_ref[...] = jax.nn.sigmoid(_dot(hb, wg_ref[...]) + bg_ref[...]).astype(BF16)


def _front(x2, B, S, g0, b0, w):
    T = B * S
    tm = TM_FRONT
    nb = S // tm
    tok = lambda b, i: (b * nb + i, 0)
    pos = lambda b, i: (i, 0)
    in_specs = [
        pl.BlockSpec((tm, D_MODEL), tok), _const_spec((1, D_MODEL)), _const_spec((1, D_MODEL)),
        _const_spec((D_MODEL, Q_LORA)), _const_spec((1, Q_LORA)), _const_spec((1, Q_LORA)),
        _const_spec((N_HEADS // 2, Q_LORA, 2 * HEAD_PAD)), _const_spec((N_HEADS // 2, Q_LORA, 2 * HEAD_PAD)),
        _const_spec((D_MODEL, KV_LORA)), _const_spec((1, KV_LORA)), _const_spec((1, KV_LORA)),
        _const_spec((N_HEADS // 2, KV_LORA, 2 * HEAD_PAD)), _const_spec((KV_LORA, N_HEADS * V_DIM)),
        _const_spec((D_MODEL, 2 * HEAD_PAD)), _const_spec((1, 2 * HEAD_PAD)),
        pl.BlockSpec((tm, HEAD_PAD), pos), pl.BlockSpec((tm, HEAD_PAD), pos),
        pl.BlockSpec((tm, HEAD_PAD), pos), pl.BlockSpec((tm, HEAD_PAD), pos),
        _const_spec((D_MODEL, F_DIM)), _const_spec((1, F_DIM)), _const_spec((F_GROUP_DIM, 2 * F_GROUP_DIM)),
        _const_spec((D_MODEL, 2 * D_MODEL)), _const_spec((1, 2 * D_MODEL)),
    ]
    out_specs = [
        pl.BlockSpec((tm, D_MODEL), tok),
        pl.BlockSpec((1, N_HEADS * HEAD_PAD, tm), lambda b, i: (b, 0, i)),
        pl.BlockSpec((1, N_HEADS, tm, HEAD_PAD), lambda b, i: (b, 0, i, 0)),
        pl.BlockSpec((1, N_HEADS * V_DIM, tm), lambda b, i: (b, 0, i)),
        pl.BlockSpec((tm, F_DIM), tok), pl.BlockSpec((tm, F_DIM), tok), pl.BlockSpec((tm, 2 * D_MODEL), tok),
    ]
    out_shape = [
        jax.ShapeDtypeStruct((T, D_MODEL), F32),
        jax.ShapeDtypeStruct((B, N_HEADS * HEAD_PAD, S), BF16),
        jax.ShapeDtypeStruct((B, N_HEADS, S, HEAD_PAD), BF16),
        jax.ShapeDtypeStruct((B, N_HEADS * V_DIM, S), BF16),
        jax.ShapeDtypeStruct((T, F_DIM), BF16), jax.ShapeDtypeStruct((T, F_DIM), BF16),
        jax.ShapeDtypeStruct((T, 2 * D_MODEL), BF16),
    ]
    return pl.pallas_call(
        _front_kernel, grid=(B, nb), in_specs=in_specs, out_specs=out_specs, out_shape=out_shape,
        compiler_params=_params(("parallel", "parallel")), name="front",
    )(x2, g0, b0, w["wq"], w["bq"], w["gq"], w["wqu"], w["wqr"], w["wkv"], w["bkv"], w["gkv"], w["wku"], w["wvu"],
      w["wkr"], w["bkr"], w["cos_q"], w["sin_q"], w["cos_k"], w["sin_k"],
      w["wz"], w["bz"], w["cs128"], w["wg"], w["bg"])


def _col_reduce(x, op, final):
    parts = [x[i:i + SUBLANES] for i in range(0, x.shape[0], SUBLANES)]
    while len(parts) > 1:
        parts = [op(parts[i], parts[i + 1]) for i in range(0, len(parts), 2)]
    return final(parts[0], axis=0, keepdims=True)


def _attn_kernel(qT_ref, k_ref, vT_ref, oT_ref, s_ref, *, tk, n_sub):
    tq = qT_ref.shape[2]
    tsub = tq // n_sub
    n_chunks = k_ref.shape[2] // tk
    subs = range(n_sub)

    def scores(c, slot):
        off = pl.multiple_of(c * tk, tk)
        k_c = k_ref[0, 0, pl.ds(off, tk), :]
        cmax = []
        for u in subs:
            s = _dot(k_c, qT_ref[0, :, u * tsub:(u + 1) * tsub])
            s_ref[slot, u] = s
            cmax.append(_col_reduce(s, jnp.maximum, jnp.max))
        return tuple(cmax)

    def softmax_pv(c, slot, carry, cmax):
        off = pl.multiple_of(c * tk, tk)
        vT_c = vT_ref[0, :, pl.ds(off, tk)]
        m_new = [jnp.maximum(carry[u][0], cmax[u]) for u in subs]
        p = [jnp.exp2(s_ref[slot, u] - m_new[u]) for u in subs]
        alpha = [jnp.exp2(carry[u][0] - m_new[u]) for u in subs]
        l = [alpha[u] * carry[u][1] + _col_reduce(p[u], jnp.add, jnp.sum) for u in subs]
        acc = [alpha[u] * carry[u][2] + _dot(vT_c, p[u].astype(BF16)) for u in subs]
        return tuple((m_new[u], l[u], acc[u]) for u in subs)

    def body(i, carry):
        state, cmax0 = carry
        c0 = 2 * i
        cmax1 = scores(c0 + 1, 1)
        state = softmax_pv(c0, 0, state, cmax0)
        cmax0 = scores(c0 + 2, 0)
        return softmax_pv(c0 + 1, 1, state, cmax1), cmax0

    cmax_first = scores(0, 0)
    init = tuple((jnp.full((1, tsub), -jnp.inf, F32), jnp.zeros((1, tsub), F32), jnp.zeros((V_DIM, tsub), F32))
                 for _ in range(n_sub))
    state, cmax0 = lax.fori_loop(0, n_chunks // 2 - 1, body, (init, cmax_first))
    cmax1 = scores(n_chunks - 1, 1)
    state = softmax_pv(n_chunks - 2, 0, state, cmax0)
    fin = softmax_pv(n_chunks - 1, 1, state, cmax1)
    for u in range(n_sub):
        _, l, acc = fin[u]
        oT_ref[0, :, u * tsub:(u + 1) * tsub] = acc * (1.0 / l)


def _attention(qT, k, vT):
    B, _, S = qT.shape
    tq = TQ_ATTN
    return pl.pallas_call(
        functools.partial(_attn_kernel, tk=min(TK_ATTN, S), n_sub=TQ_ATTN // TQ_SUB),
        grid=(B, N_HEADS, S // tq),
        in_specs=[
            pl.BlockSpec((1, HEAD_PAD, tq), lambda b, h, i: (b, h, i)),
            pl.BlockSpec((1, 1, S, HEAD_PAD), lambda b, h, i: (b, h, 0, 0)),
            pl.BlockSpec((1, V_DIM, S), lambda b, h, i: (b, h, 0)),
        ],
        out_specs=pl.BlockSpec((1, V_DIM, tq), lambda b, h, i: (b, h, i)),
        out_shape=jax.ShapeDtypeStruct((B, N_HEADS * V_DIM, S), F32),
        scratch_shapes=[pltpu.VMEM((2, TQ_ATTN // TQ_SUB, min(TK_ATTN, S), TQ_SUB), F32)],
        compiler_params=_params(("parallel", "parallel", "parallel")),
        name="attention",
    )(qT, k, vT)


def _fft_a_kernel(ur_ref, ui_ref, m1_ref, y_ref):
    u = jnp.concatenate([ur_ref[0], ui_ref[0]], axis=0)
    y_ref[0] = _dot(m1_ref[...], u).astype(BF16)


def _fft_a(ur3, ui3, m1):
    B, n1, W = ur3.shape
    L = min(FFT_A_LANES, W)
    blk = lambda b, j: (b, 0, j)
    return pl.pallas_call(
        _fft_a_kernel, grid=(B, W // L),
        in_specs=[pl.BlockSpec((1, n1, L), blk), pl.BlockSpec((1, n1, L), blk), _const_spec((2 * n1, 2 * n1))],
        out_specs=pl.BlockSpec((1, 2 * n1, L), blk),
        out_shape=jax.ShapeDtypeStruct((B, 2 * n1, W), BF16),
        compiler_params=_params(("parallel", "parallel")), name="fft_a",
    )(ur3, ui3, m1)


def _fft_b_kernel(y_ref, g_ref, o_ref, *, nk, scale):
    for j in range(nk):
        ycat = jnp.concatenate([y_ref[0, 0, j], y_ref[0, 1, j]], axis=0)
        o_ref[0, :, j * F_DIM:(j + 1) * F_DIM] = (_dot(g_ref[j], ycat) * scale).astype(BF16)


def _fft_b(y5, gtab, scale):
    B, _, n1, _, _ = y5.shape
    nk = min(FFT_B_K1, n1)
    return pl.pallas_call(
        functools.partial(_fft_b_kernel, nk=nk, scale=scale), grid=(B, n1 // nk),
        in_specs=[pl.BlockSpec((1, 2, nk, LANES, F_DIM), lambda b, j: (b, 0, j, 0, 0)),
                  pl.BlockSpec((nk, LANES, 2 * LANES), lambda b, j: (j, 0, 0))],
        out_specs=pl.BlockSpec((1, LANES, nk * F_DIM), lambda b, j: (b, 0, j)),
        out_shape=jax.ShapeDtypeStruct((B, LANES, n1 * F_DIM), BF16),
        compiler_params=_params(("parallel", "parallel")), name="fft_b",
    )(y5, gtab)


def _post_kernel(oT_ref, yf_ref, g_ref, h0_ref, wo_ref, wf_ref, wout_ref, bout_ref, lg_ref, lb_ref, h1_ref, h1T_ref):
    o = oT_ref[0].T.astype(BF16)
    y_a = _dot(o, wo_ref[...])
    y_f = _dot(yf_ref[...], wf_ref[...])
    g = g_ref[...].astype(F32)
    m = (g[:, :D_MODEL] * y_a + g[:, D_MODEL:] * y_f).astype(BF16)
    mix = _dot(m, wout_ref[...]) + bout_ref[...]
    h1 = _layer_norm(DN_ALPHA * h0_ref[...] + mix, lg_ref[...], lb_ref[...])
    h1_ref[...] = h1
    h1T_ref[...] = h1.T.astype(BF16)


def _post(oT, yf, g, h0, B, S, w):
    T = B * S
    tm = TM_POST
    nb = S // tm
    tok = lambda b, i: (b * nb + i, 0)
    return pl.pallas_call(
        _post_kernel, grid=(B, nb),
        in_specs=[pl.BlockSpec((1, N_HEADS * V_DIM, tm), lambda b, i: (b, 0, i)),
                  pl.BlockSpec((tm, F_DIM), tok), pl.BlockSpec((tm, 2 * D_MODEL), tok), pl.BlockSpec((tm, D_MODEL), tok),
                  _const_spec((N_HEADS * V_DIM, D_MODEL)), _const_spec((F_DIM, D_MODEL)), _const_spec((D_MODEL, D_MODEL)),
                  _const_spec((1, D_MODEL)), _const_spec((1, D_MODEL)), _const_spec((1, D_MODEL))],
        out_specs=[pl.BlockSpec((tm, D_MODEL), tok), pl.BlockSpec((D_MODEL, tm), lambda b, i: (0, b * nb + i))],
        out_shape=[jax.ShapeDtypeStruct((T, D_MODEL), F32), jax.ShapeDtypeStruct((D_MODEL, T), BF16)],
        compiler_params=_params(("parallel", "parallel")), name="post",
    )(oT, yf, g, h0, w["wo"], w["wf"], w["wout"], w["bout"], w["ln1_g"], w["ln1_b"])


def _extract_top16(s):
    row = lax.broadcasted_iota(jnp.int32, s.shape, 0).astype(F32)
    slot = lax.broadcasted_iota(jnp.int32, (PEER_TOPK, s.shape[1]), 0)
    rank = jnp.full(s.shape, float(PEER_TOPK), F32)
    vals = jnp.zeros((PEER_TOPK, s.shape[1]), F32)
    for r in range(PEER_TOPK):
        m = jnp.max(s, axis=0, keepdims=True)
        idx = jnp.min(jnp.where(s == m, row, float(PEER_NKEYS)), axis=0, keepdims=True)
        hit = row == idx
        s = jnp.where(hit, -jnp.inf, s)
        rank = jnp.where(hit, float(r), rank)
        vals = jnp.where(slot == r, m, vals)
    return vals, rank


def _cand_rows():
    groups = [[(i, j0 + r) if r < valid else None for r in range(SUBLANES)] for (i, j0, valid) in _CAND_GROUPS]
    groups.append([(SUBLANES + r, 0) for r in range(SUBLANES)])
    return groups


def _static_beats(cp, c):
    if cp == c:
        return 0
    if cp[0] <= c[0] and cp[1] <= c[1]:
        return 1
    if cp[0] >= c[0] and cp[1] >= c[1]:
        return 0
    return None


def _static_counts():
    rows = _cand_rows()
    out = [[0.0] * SUBLANES for _ in rows]
    for cg in rows:
        for cp in cg:
            if cp is None:
                continue
            for g, tgt in enumerate(rows):
                res = [_static_beats(cp, c) if c is not None else 0 for c in tgt]
                if all(v is not None for v in res):
                    for r in range(SUBLANES):
                        out[g][r] += float(res[r])
    return out


def _select_pairs(a_rep, b_lo, b_hi, a_hi, static_counts):
    L = b_lo.shape[1]
    sub = lax.broadcasted_iota(jnp.int32, (SUBLANES, L), 0)
    rows = _cand_rows()
    groups = []
    for (i, j0, valid) in _CAND_GROUPS:
        v = a_rep[i] + (b_lo if j0 == 0 else b_hi)
        if valid < SUBLANES:
            v = jnp.where(sub < valid, v, -jnp.inf)
        groups.append(v)
    groups.append(a_hi + jnp.broadcast_to(b_lo[0:1, :], (SUBLANES, L)))
    n_groups = len(groups)
    counts = [static_counts[g] for g in range(n_groups)]
    for gp in range(n_groups):
        for rp in range(SUBLANES):
            cp = rows[gp][rp]
            if cp is None:
                continue
            vb = jnp.broadcast_to(groups[gp][rp:rp + 1, :], (SUBLANES, L))
            for g in range(n_groups):
                if all(c is None or _static_beats(cp, c) is not None for c in rows[g]):
                    continue
                if g < gp:
                    beats = jnp.where(vb > groups[g], 1.0, 0.0)
                elif g > gp:
                    beats = jnp.where(vb >= groups[g], 1.0, 0.0)
                else:
                    beats = jnp.where(sub > rp, jnp.where(vb >= groups[g], 1.0, 0.0), jnp.where(vb > groups[g], 1.0, 0.0))
                counts[g] = counts[g] + beats
    top = groups[0][0:1, :]
    valid_rows = [g[2] for g in _CAND_GROUPS] + [SUBLANES]
    sel = []
    z = jnp.zeros((1, L), F32)
    for g in range(n_groups):
        s_g = jnp.where(counts[g] < float(PEER_TOPK), 1.0, 0.0)
        if valid_rows[g] < SUBLANES:
            s_g = jnp.where(sub < valid_rows[g], s_g, 0.0)
        sel.append(s_g)
        z = z + jnp.sum(s_g * jnp.exp(groups[g] - top), axis=0, keepdims=True)
    n = [jnp.sum(sel[0] + sel[1], axis=0, keepdims=True)]
    for g in range(2, n_groups - 1):
        n.append(jnp.sum(sel[g], axis=0, keepdims=True))
    for r in range(SUBLANES):
        n.append(sel[n_groups - 1][r:r + 1, :])
    return n, z


def _rows_from_rep(rep, lo):
    sub = lax.broadcasted_iota(jnp.int32, rep[0].shape, 0)
    out = rep[lo]
    for r in range(1, SUBLANES):
        out = jnp.where(sub == r, rep[lo + r], out)
    return out


def _oddeven_merge_pairs(n):
    pairs = []
    t = n.bit_length() - 1
    for pi in range(t):
        p = 1 << pi
        for ki in range(pi, -1, -1):
            k = 1 << ki
            for j in range(k % p, n - k, 2 * k):
                for i in range(min(k, n - j - k)):
                    if (i + j) // (2 * p) == (i + j + k) // (2 * p):
                        pairs.append((i + j, i + j + k))
    return pairs


_SORT16 = _oddeven_merge_pairs(PEER_TOPK)


def _top16_values(s):
    w = [s[g * SUBLANES:(g + 1) * SUBLANES] for g in range(PEER_NKEYS // SUBLANES)]
    for (i, j) in _SORT16:
        w[i], w[j] = jnp.maximum(w[i], w[j]), jnp.minimum(w[i], w[j])
    for shift in (4, 2, 1):
        other = [pltpu.roll(x, shift, axis=0) for x in w]
        w = [jnp.maximum(w[i], other[PEER_TOPK - 1 - i]) for i in range(PEER_TOPK)]
        d = PEER_TOPK // 2
        while d >= 1:
            for i in range(PEER_TOPK):
                if i & d == 0:
                    w[i], w[i + d] = jnp.maximum(w[i], w[i + d]), jnp.minimum(w[i], w[i + d])
            d //= 2
    return w


def _tie_flags(s, w):
    flag = jnp.zeros_like(w[0])
    for r in range(PEER_TOPK - 1):
        flag = flag + jnp.where(w[r] == w[r + 1], 1.0, 0.0)
    parts = [jnp.where(s[g * SUBLANES:(g + 1) * SUBLANES] >= w[PEER_TOPK - 1], 1.0, 0.0)
             for g in range(PEER_NKEYS // SUBLANES)]
    while len(parts) > 1:
        parts = [parts[i] + parts[i + 1] for i in range(0, len(parts), 2)]
    count = jnp.sum(parts[0], axis=0, keepdims=True)
    return flag + (count - float(PEER_TOPK))


def _peer_topk_kernel(h1T_ref, wqT_ref, keys_ref, sc_ref, n_ref, c_ref, r2_ref, e2_ref, qp_ref):
    tm = h1T_ref.shape[1]
    qp_ref[...] = _dot(wqT_ref[...], h1T_ref[...]).astype(BF16)

    def scores(h, lo):
        r0, r1 = 2 * h * PEER_HALF, (2 * h + 1) * PEER_HALF
        if not isinstance(h, int):
            r0, r1 = pl.multiple_of(r0, PEER_HALF), pl.multiple_of(r1, PEER_HALF)
        s1 = _dot(keys_ref[2 * h], qp_ref[pl.ds(r0, PEER_HALF), pl.ds(lo, LANES)])
        s2 = _dot(keys_ref[2 * h + 1], qp_ref[pl.ds(r1, PEER_HALF), pl.ds(lo, LANES)])
        return s1, s2

    def store(h, lo, n_a, c_a, rank2, e2):
        n_ref[h, :, pl.ds(lo, LANES)] = n_a
        c_ref[h, :, pl.ds(lo, LANES)] = c_a
        r2_ref[h, :, pl.ds(lo, LANES)] = rank2.astype(BF16)
        e2_ref[h, :, pl.ds(lo, LANES)] = e2.astype(BF16)

    def fast_head(h, lo):
        s1, s2 = scores(h, lo)
        w1, w2 = _top16_values(s1), _top16_values(s2)
        n, z = _select_pairs(w1, _rows_from_rep(w2, 0), _rows_from_rep(w2, SUBLANES), _rows_from_rep(w1, SUBLANES),
                             sc_ref)
        n_parts, r_parts = [], []
        for g in range(PEER_NKEYS // SUBLANES):
            s1_g, s2_g = s1[g * SUBLANES:(g + 1) * SUBLANES], s2[g * SUBLANES:(g + 1) * SUBLANES]
            n_g = jnp.zeros_like(s1_g)
            r_g = jnp.full_like(s2_g, float(PEER_TOPK))
            for r in range(PEER_TOPK):
                n_g = jnp.where(s1_g == w1[r], n[r], n_g)
                r_g = jnp.where(s2_g == w2[r], float(r), r_g)
            n_parts.append(n_g)
            r_parts.append(r_g)
        c_a = jnp.exp(s1 - w1[0][0:1, :]) * (GELU_HALF / z)
        e2 = jnp.exp(s2 - w2[0][0:1, :])
        store(h, lo, jnp.concatenate(n_parts, axis=0), c_a, jnp.concatenate(r_parts, axis=0), e2)
        return _tie_flags(s1, w1) + _tie_flags(s2, w2)

    def exact_head(h, lo):
        s1, s2 = scores(h, lo)
        a_vals, rank1 = _extract_top16(s1)
        b_vals, rank2 = _extract_top16(s2)
        a_rep = [jnp.broadcast_to(a_vals[i:i + 1, :], (SUBLANES, LANES)) for i in range(PEER_TOPK)]
        n, z = _select_pairs(a_rep, b_vals[:SUBLANES], b_vals[SUBLANES:], a_vals[SUBLANES:], sc_ref)
        n_a = jnp.zeros_like(s1)
        for i in range(PEER_TOPK):
            n_a = n_a + jnp.where(rank1 == float(i), n[i], 0.0)
        c_a = jnp.exp(s1 - a_vals[0:1, :]) * (GELU_HALF / z)
        e2 = jnp.exp(s2 - b_vals[0:1, :])
        store(h, lo, n_a, c_a, rank2, e2)

    def chunk(ci, _):
        lo = pl.multiple_of(ci * LANES, LANES)
        flags = jnp.zeros((SUBLANES, LANES), F32)
        for h in range(PEER_HEADS):
            flags = flags + fast_head(h, lo)

        @pl.when(jnp.max(flags) > 0.0)
        def _():
            def per_head(h, carry):
                exact_head(h, lo)
                return carry
            lax.fori_loop(0, PEER_HEADS, per_head, 0)
        return 0

    lax.fori_loop(0, tm // LANES, chunk, 0)


def _peer_topk(h1T, wqT, keys):
    T = h1T.shape[1]
    tm = TM_TOPK
    nq = wqT.shape[0]
    col = lambda i: (0, i)
    col3 = lambda i: (0, 0, i)
    static = jnp.broadcast_to(jnp.asarray(_static_counts(), F32)[:, :, None], (len(_CAND_GROUPS) + 1, SUBLANES, LANES))
    return pl.pallas_call(
        _peer_topk_kernel, grid=(T // tm,),
        in_specs=[pl.BlockSpec((D_MODEL, tm), col), _const_spec((nq, D_MODEL)),
                  _const_spec((2 * PEER_HEADS, PEER_NKEYS, PEER_HALF)), _const_spec(static.shape)],
        out_specs=[pl.BlockSpec((PEER_HEADS, PEER_NKEYS, tm), col3), pl.BlockSpec((PEER_HEADS, PEER_NKEYS, tm), col3),
                   pl.BlockSpec((PEER_HEADS, PEER_NKEYS, tm), col3), pl.BlockSpec((PEER_HEADS, PEER_NKEYS, tm), col3)],
        out_shape=[jax.ShapeDtypeStruct((PEER_HEADS, PEER_NKEYS, T), F32),
                   jax.ShapeDtypeStruct((PEER_HEADS, PEER_NKEYS, T), F32),
                   jax.ShapeDtypeStruct((PEER_HEADS, PEER_NKEYS, T), BF16),
                   jax.ShapeDtypeStruct((PEER_HEADS, PEER_NKEYS, T), BF16)],
        scratch_shapes=[pltpu.VMEM((nq, tm), BF16)],
        compiler_params=_params(("parallel",)), name="peer_topk",
    )(h1T, wqT, keys, static)


def _peer_dense_kernel(xT_ref, wu_ref, wvT_ref, n_ref, c_ref, r2_ref, e2_ref, h1_ref, lg_ref, lb_ref, out_ref,
                       acc_ref, *, eb):
    j = pl.program_id(1)
    tm = xT_ref.shape[1]

    @pl.when(j == 0)
    def _():
        acc_ref[...] = jnp.zeros_like(acc_ref)

    n_chains = eb // PEER_CHAIN
    rows = [slice(ch * PEER_CHAIN, (ch + 1) * PEER_CHAIN) for ch in range(n_chains)]

    def first_matmul(ch):
        return _dot(wu_ref[rows[ch], :], xT_ref[...])

    def gate_gelu(ch, hpre):
        acts = []
        for aa in range(PEER_CHAIN // PEER_NKEYS):
            a_key = (j * n_chains + ch) * (PEER_CHAIN // PEER_NKEYS) + aa
            gate = jnp.zeros((PEER_NKEYS, tm), BF16)
            for h in range(PEER_HEADS):
                n_b = jnp.broadcast_to(n_ref[h, pl.ds(a_key, 1), :], (PEER_NKEYS, tm)).astype(BF16)
                c_b = jnp.broadcast_to(c_ref[h, pl.ds(a_key, 1), :], (PEER_NKEYS, tm)).astype(BF16)
                gate = gate + jnp.where(r2_ref[h] < n_b, e2_ref[h], jnp.zeros((), BF16)) * c_b
            hp = hpre[aa * PEER_NKEYS:(aa + 1) * PEER_NKEYS, :]
            act = hp * (1.0 + lax.erf(hp * (1.0 / math.sqrt(2.0))))
            acts.append(act.astype(BF16) * gate)
        return jnp.concatenate(acts, axis=0)

    hpre, act = {}, {}
    for t in range(n_chains + 2):
        if t < n_chains:
            hpre[t] = first_matmul(t)
        if 0 <= t - 1 < n_chains:
            act[t - 1] = gate_gelu(t - 1, hpre.pop(t - 1))
        if 0 <= t - 2 < n_chains:
            acc_ref[...] += _dot(wvT_ref[:, rows[t - 2]], act.pop(t - 2))

    @pl.when(j == pl.num_programs(1) - 1)
    def _():
        y = acc_ref[...].T
        out_ref[...] = _layer_norm(DN_ALPHA * h1_ref[...] + y, lg_ref[...], lb_ref[...])


def _peer_dense(h1T, wu, wvT, n_tab, c_tab, r2, e2, h1, lg, lb):
    T = h1T.shape[1]
    tm, eb = TM_PEER, EB_PEER
    col = lambda i, j: (0, i)
    col3 = lambda i, j: (0, 0, i)
    row = lambda i, j: (i, 0)
    return pl.pallas_call(
        functools.partial(_peer_dense_kernel, eb=eb), grid=(T // tm, PEER_EXPERTS // eb),
        in_specs=[pl.BlockSpec((D_MODEL, tm), col),
                  pl.BlockSpec((eb, D_MODEL), lambda i, j: (j, 0)),
                  pl.BlockSpec((D_MODEL, eb), lambda i, j: (0, j)),
                  pl.BlockSpec((PEER_HEADS, PEER_NKEYS, tm), col3), pl.BlockSpec((PEER_HEADS, PEER_NKEYS, tm), col3),
                  pl.BlockSpec((PEER_HEADS, PEER_NKEYS, tm), col3), pl.BlockSpec((PEER_HEADS, PEER_NKEYS, tm), col3),
                  pl.BlockSpec((tm, D_MODEL), row), _const_spec((1, D_MODEL)), _const_spec((1, D_MODEL))],
        out_specs=pl.BlockSpec((tm, D_MODEL), row),
        out_shape=jax.ShapeDtypeStruct((T, D_MODEL), F32),
        scratch_shapes=[pltpu.VMEM((D_MODEL, tm), F32)],
        compiler_params=_params(("parallel", "arbitrary")), name="peer_dense",
    )(h1T, wu, wvT, n_tab, c_tab, r2, e2, h1, lg, lb)


def _transpose_cast_kernel(w_ref, o_ref):
    o_ref[...] = w_ref[...].T.astype(BF16)


def _transpose_cast(w):
    rows, d = w.shape
    tr = TM_WPREP
    return pl.pallas_call(
        _transpose_cast_kernel, grid=(rows // tr,),
        in_specs=[pl.BlockSpec((tr, d), lambda i: (i, 0))],
        out_specs=pl.BlockSpec((d, tr), lambda i: (0, i)),
        out_shape=jax.ShapeDtypeStruct((d, rows), BF16),
        compiler_params=_params(("parallel",)), name="wv_prep",
    )(w)


def _pad_cols(w, lo, total):
    return jnp.pad(w, ((0, 0), (lo, total - lo - w.shape[1])))


def _prepare(S, w_in, b_in, q_norm_g, kv_norm_g, w_uq, w_uk, w_uv, w_o_attn, w_fourier, w_out, b_out, ln1_g, ln1_b,
             peer_w_q, peer_keys, peer_w_u, peer_w_v):
    o_kv, o_kr, o_z, o_g = Q_LORA, Q_LORA + KV_LORA, Q_LORA + KV_LORA + QK_ROPE_DIM, Q_LORA + KV_LORA + QK_ROPE_DIM + F_DIM
    row = lambda v: v.reshape(1, -1).astype(F32)
    w = {}
    w["wq"], w["bq"], w["gq"] = w_in[:, :o_kv].astype(BF16), row(b_in[:o_kv]), row(q_norm_g)
    w["wkv"], w["bkv"], w["gkv"] = w_in[:, o_kv:o_kr].astype(BF16), row(b_in[o_kv:o_kr]), row(kv_norm_g)
    wkr, bkr = w_in[:, o_kr:o_z], b_in[o_kr:o_z].reshape(1, -1)
    swap = lambda m: jnp.concatenate([m[:, ROPE_HALF:], m[:, :ROPE_HALF]], axis=1)
    w["wkr"] = jnp.concatenate([_pad_cols(wkr, QK_NOPE_DIM, HEAD_PAD), _pad_cols(swap(wkr), QK_NOPE_DIM, HEAD_PAD)],
                               axis=1).astype(BF16)
    w["bkr"] = jnp.concatenate([_pad_cols(bkr, QK_NOPE_DIM, HEAD_PAD), _pad_cols(swap(bkr), QK_NOPE_DIM, HEAD_PAD)],
                               axis=1).astype(F32)
    w["wz"], w["bz"] = w_in[:, o_z:o_g].astype(BF16), row(b_in[o_z:o_g])
    w["wg"], w["bg"] = w_in[:, o_g:].astype(BF16), row(b_in[o_g:])
    wq3 = w_uq.reshape(Q_LORA, N_HEADS, QK_DIM).transpose(1, 0, 2)
    rope3 = wq3[:, :, QK_NOPE_DIM:]
    rope3_sw = jnp.concatenate([rope3[:, :, ROPE_HALF:], rope3[:, :, :ROPE_HALF]], axis=2)
    padh = lambda m: jnp.pad(m, ((0, 0), (0, 0), (0, HEAD_PAD - m.shape[2])))
    pair = lambda m: m.reshape(N_HEADS // 2, 2, m.shape[1], HEAD_PAD).transpose(0, 2, 1, 3).reshape(
        N_HEADS // 2, m.shape[1], 2 * HEAD_PAD)
    w["wqu"] = pair(padh(wq3)).astype(BF16)
    w["wqr"] = pair(padh(jnp.concatenate([jnp.zeros_like(wq3[:, :, :QK_NOPE_DIM]), rope3_sw], axis=2))).astype(BF16)
    wk3 = w_uk.reshape(KV_LORA, N_HEADS, QK_NOPE_DIM).transpose(1, 0, 2)
    w["wku"] = pair(padh(wk3)).astype(BF16)
    w["wvu"] = w_uv.astype(BF16)
    pos = jnp.arange(S, dtype=F32)
    inv = 1.0 / (ROPE_THETA ** (jnp.arange(0, QK_ROPE_DIM, 2, dtype=F32) / QK_ROPE_DIM))
    ang = pos[:, None] * inv[None, :]
    cos, sin = jnp.cos(ang), jnp.sin(ang)
    ones, zeros = jnp.ones((S, QK_NOPE_DIM), F32), jnp.zeros((S, QK_NOPE_DIM), F32)
    tail = jnp.zeros((S, HEAD_PAD - QK_DIM), F32)
    scale = QK_DIM ** -0.5 * math.log2(math.e)
    w["cos_q"] = jnp.concatenate([ones, cos, cos, tail], axis=1) * scale
    w["sin_q"] = jnp.concatenate([zeros, -sin, sin, tail], axis=1) * scale
    w["cos_k"] = jnp.concatenate([zeros, cos, cos, tail], axis=1)
    w["sin_k"] = jnp.concatenate([zeros, -sin, sin, tail], axis=1)
    n1 = S // LANES
    idx = lambda n: jnp.arange(n, dtype=jnp.int32)
    ang_c = (2.0 * math.pi / F_GROUP_DIM) * ((idx(F_GROUP_DIM)[:, None] * idx(F_GROUP_DIM)[None, :]) % F_GROUP_DIM).astype(F32)
    w["cs128"] = jnp.concatenate([jnp.cos(ang_c), jnp.sin(ang_c)], axis=1).astype(BF16)
    ang_1 = (2.0 * math.pi / n1) * ((idx(n1)[:, None] * idx(n1)[None, :]) % n1).astype(F32)
    c1, s1 = jnp.cos(ang_1), jnp.sin(ang_1)
    w["m1"] = jnp.concatenate([jnp.concatenate([c1, s1], axis=1), jnp.concatenate([-s1, c1], axis=1)], axis=0).astype(BF16)
    kk = idx(n1)[:, None, None] + n1 * idx(LANES)[None, :, None]
    ang_g = (2.0 * math.pi / S) * ((kk * idx(LANES)[None, None, :]) % S).astype(F32)
    w["gtab"] = jnp.concatenate([jnp.cos(ang_g), jnp.sin(ang_g)], axis=2).astype(BF16)
    w["wo"], w["wf"], w["wout"] = w_o_attn.astype(BF16), w_fourier.astype(BF16), w_out.astype(BF16)
    w["bout"], w["ln1_g"], w["ln1_b"] = row(b_out), row(ln1_g), row(ln1_b)
    w["wqT"] = peer_w_q.astype(BF16).T
    w["keys"] = peer_keys.reshape(2 * PEER_HEADS, PEER_NKEYS, PEER_HALF).astype(BF16)
    w["wu"] = peer_w_u.astype(BF16)
    w["wvT"] = _transpose_cast(peer_w_v)
    return w


def kernel(x, ln0_g, ln0_b, w_in, b_in, q_norm_g, kv_norm_g, w_uq, w_uk, w_uv, w_o_attn, w_fourier, w_out, b_out, ln1_g,
           ln1_b, peer_w_q, peer_keys, peer_w_u, peer_w_v, ln2_g, ln2_b):
    B, S, D = x.shape
    assert D == D_MODEL and w_in.shape[0] == DEPTH
    T = B * S
    n1 = S // LANES
    assert S % max(TM_FRONT, TQ_ATTN, TM_POST, LANES * SUBLANES) == 0 and T % max(TM_PEER, TM_TOPK) == 0
    row = lambda v: v.reshape(1, -1).astype(F32)
    h = x.reshape(T, D)
    w = _prepare(S, w_in[0], b_in[0], q_norm_g[0], kv_norm_g[0], w_uq[0], w_uk[0], w_uv[0], w_o_attn[0], w_fourier[0],
                 w_out[0], b_out[0], ln1_g[0], ln1_b[0], peer_w_q[0], peer_keys[0], peer_w_u[0], peer_w_v[0])
    h0, qT, k, vT, ur, ui, g = _front(h, B, S, row(ln0_g), row(ln0_b), w)
    oT = _attention(qT, k, vT)
    y = _fft_a(ur.reshape(B, n1, LANES * F_DIM), ui.reshape(B, n1, LANES * F_DIM), w["m1"])
    yf = _fft_b(y.reshape(B, 2, n1, LANES, F_DIM), w["gtab"], 1.0 / math.sqrt(S * F_GROUP_DIM))
    h1, h1T = _post(oT, yf.reshape(T, F_DIM), g, h0, B, S, w)
    n_tab, c_tab, r2, e2 = _peer_topk(h1T, w["wqT"], w["keys"])
    out = _peer_dense(h1T, w["wu"], w["wvT"], n_tab, c_tab, r2, e2, h1, row(ln2_g[0]), row(ln2_b[0]))
    return out.reshape(B, S, D)
```

```python
import functools
import math

import jax
import jax.numpy as jnp
from jax import lax
from jax.experimental import pallas as pl
from jax.experimental.pallas import tpu as pltpu

F32 = jnp.float32
BF16 = jnp.bfloat16

D_MODEL = 1024
N_HEADS = 8
QK_NOPE_DIM = 64
QK_ROPE_DIM = 32
ROPE_HALF = QK_ROPE_DIM // 2
QK_DIM = QK_NOPE_DIM + QK_ROPE_DIM
V_DIM = 64
Q_LORA = 256
KV_LORA = 256
ROPE_THETA = 10000.0
F_GROUPS = 4
F_GROUP_DIM = 128
F_DIM = F_GROUPS * F_GROUP_DIM
PEER_HEADS = 8
PEER_NKEYS = 128
PEER_EXPERTS = PEER_NKEYS * PEER_NKEYS
PEER_HALF = 128
PEER_TOPK = 16
DEPTH = 1
DN_ALPHA = (2.0 * DEPTH) ** 0.25
LN_EPS = 1e-5
RMS_EPS = 1e-6
GELU_HALF = 0.5

LANES = 128
SUBLANES = 8
HEAD_PAD = LANES
VMEM_LIMIT_BYTES = 56 * 1024 * 1024

TM_FRONT = 512
TQ_ATTN = 2048
TQ_SUB = 256
TK_ATTN = 512
FFT_A_LANES = 8192
FFT_B_K1 = 16
TM_POST = 512
TM_TOPK = 512
TM_WPREP = 2048
TM_PEER = 512
EB_PEER = 2048
PEER_CHAIN = 512

_CAND_GROUPS = ((0, 0, 8), (0, 8, 8), (1, 0, 8), (2, 0, 5), (3, 0, 4), (4, 0, 3), (5, 0, 2), (6, 0, 2), (7, 0, 2))


def _params(sem):
    return pltpu.CompilerParams(dimension_semantics=sem, vmem_limit_bytes=VMEM_LIMIT_BYTES)


def _const_spec(shape):
    nd = len(shape)
    return pl.BlockSpec(shape, lambda *_: (0,) * nd)


def _layer_norm(x, g, b):
    mu = jnp.mean(x, axis=-1, keepdims=True)
    xc = x - mu
    var = jnp.mean(xc * xc, axis=-1, keepdims=True)
    return xc * lax.rsqrt(var + LN_EPS) * g + b


def _rms_norm(x, g):
    return x * lax.rsqrt(jnp.mean(x * x, axis=-1, keepdims=True) + RMS_EPS) * g


def _dot(a, b):
    return jnp.dot(a, b, preferred_element_type=F32)


def _front_kernel(x_ref, g0_ref, b0_ref, wq_ref, bq_ref, gq_ref, wqu_ref, wqr_ref, wkv_ref, bkv_ref, gkv_ref, wku_ref,
                  wvu_ref, wkr_ref, bkr_ref, cq_ref, sq_ref, ck_ref, sk_ref,
                  wz_ref, bz_ref, cs_ref, wg_ref, bg_ref,
                  h0_ref, qT_ref, k_ref, vT_ref, ur_ref, ui_ref, g_ref):
    h0 = _layer_norm(x_ref[...], g0_ref[...], b0_ref[...])
    h0_ref[...] = h0
    hb = h0.astype(BF16)
    cq = _rms_norm(_dot(hb, wq_ref[...]) + bq_ref[...], gq_ref[...]).astype(BF16)
    ckv = _rms_norm(_dot(hb, wkv_ref[...]) + bkv_ref[...], gkv_ref[...]).astype(BF16)
    cos_q, sin_q = cq_ref[...], sq_ref[...]
    kr2 = _dot(hb, wkr_ref[...]) + bkr_ref[...]
    k_rope = kr2[:, :HEAD_PAD] * ck_ref[...] + kr2[:, HEAD_PAD:] * sk_ref[...]
    cos_q2, sin_q2 = jnp.concatenate([cos_q, cos_q], axis=1), jnp.concatenate([sin_q, sin_q], axis=1)
    k_rope2 = jnp.concatenate([k_rope, k_rope], axis=1)
    for hp in range(N_HEADS // 2):
        q_p = _dot(cq, wqu_ref[hp]) * cos_q2 + _dot(cq, wqr_ref[hp]) * sin_q2
        k_p = (_dot(ckv, wku_ref[hp]) + k_rope2).astype(BF16)
        for hh in range(2):
            h = 2 * hp + hh
            lanes = slice(hh * HEAD_PAD, (hh + 1) * HEAD_PAD)
            qT_ref[0, h * HEAD_PAD:(h + 1) * HEAD_PAD, :] = q_p[:, lanes].T.astype(BF16)
            k_ref[0, h] = k_p[:, lanes]
    v_all = _dot(ckv, wvu_ref[...])
    vT_ref[0] = v_all.T.astype(BF16)
    z = (_dot(hb, wz_ref[...]) + bz_ref[...]).astype(BF16)
    cs = cs_ref[...]
    for gi in range(F_GROUPS):
        lo, hi = gi * F_GROUP_DIM, (gi + 1) * F_GROUP_DIM
        pq = _dot(z[:, lo:hi], cs)
        ur_ref[:, lo:hi] = pq[:, :F_GROUP_DIM].astype(BF16)
        ui_ref[:, lo:hi] = (-pq[:, F_GROUP_DIM:]).astype(BF16)
    g_ref[...] = jax.nn.sigmoid(_dot(hb, wg_ref[...]) + bg_ref[...]).astype(BF16)


def _front(x2, B, S, g0, b0, w):
    T = B * S
    tm = TM_FRONT
    nb = S // tm
    tok = lambda b, i: (b * nb + i, 0)
    pos = lambda b, i: (i, 0)
    in_specs = [
        pl.BlockSpec((tm, D_MODEL), tok), _const_spec((1, D_MODEL)), _const_spec((1, D_MODEL)),
        _const_spec((D_MODEL, Q_LORA)), _const_spec((1, Q_LORA)), _const_spec((1, Q_LORA)),
        _const_spec((N_HEADS // 2, Q_LORA, 2 * HEAD_PAD)), _const_spec((N_HEADS // 2, Q_LORA, 2 * HEAD_PAD)),
        _const_spec((D_MODEL, KV_LORA)), _const_spec((1, KV_LORA)), _const_spec((1, KV_LORA)),
        _const_spec((N_HEADS // 2, KV_LORA, 2 * HEAD_PAD)), _const_spec((KV_LORA, N_HEADS * V_DIM)),
        _const_spec((D_MODEL, 2 * HEAD_PAD)), _const_spec((1, 2 * HEAD_PAD)),
        pl.BlockSpec((tm, HEAD_PAD), pos), pl.BlockSpec((tm, HEAD_PAD), pos),
        pl.BlockSpec((tm, HEAD_PAD), pos), pl.BlockSpec((tm, HEAD_PAD), pos),
        _const_spec((D_MODEL, F_DIM)), _const_spec((1, F_DIM)), _const_spec((F_GROUP_DIM, 2 * F_GROUP_DIM)),
        _const_spec((D_MODEL, 2 * D_MODEL)), _const_spec((1, 2 * D_MODEL)),
    ]
    out_specs = [
        pl.BlockSpec((tm, D_MODEL), tok),
        pl.BlockSpec((1, N_HEADS * HEAD_PAD, tm), lambda b, i: (b, 0, i)),
        pl.BlockSpec((1, N_HEADS, tm, HEAD_PAD), lambda b, i: (b, 0, i, 0)),
        pl.BlockSpec((1, N_HEADS * V_DIM, tm), lambda b, i: (b, 0, i)),
        pl.BlockSpec((tm, F_DIM), tok), pl.BlockSpec((tm, F_DIM), tok), pl.BlockSpec((tm, 2 * D_MODEL), tok),
    ]
    out_shape = [
        jax.ShapeDtypeStruct((T, D_MODEL), F32),
        jax.ShapeDtypeStruct((B, N_HEADS * HEAD_PAD, S), BF16),
        jax.ShapeDtypeStruct((B, N_HEADS, S, HEAD_PAD), BF16),
        jax.ShapeDtypeStruct((B, N_HEADS * V_DIM, S), BF16),
        jax.ShapeDtypeStruct((T, F_DIM), BF16), jax.ShapeDtypeStruct((T, F_DIM), BF16),
        jax.ShapeDtypeStruct((T, 2 * D_MODEL), BF16),
    ]
    return pl.pallas_call(
        _front_kernel, grid=(B, nb), in_specs=in_specs, out_specs=out_specs, out_shape=out_shape,
        compiler_params=_params(("parallel", "parallel")), name="front",
    )(x2, g0, b0, w["wq"], w["bq"], w["gq"], w["wqu"], w["wqr"], w["wkv"], w["bkv"], w["gkv"], w["wku"], w["wvu"],
      w["wkr"], w["bkr"], w["cos_q"], w["sin_q"], w["cos_k"], w["sin_k"],
      w["wz"], w["bz"], w["cs128"], w["wg"], w["bg"])


def _col_reduce(x, op, final):
    parts = [x[i:i + SUBLANES] for i in range(0, x.shape[0], SUBLANES)]
    while len(parts) > 1:
        parts = [op(parts[i], parts[i + 1]) for i in range(0, len(parts), 2)]
    return final(parts[0], axis=0, keepdims=True)


def _attn_kernel(qT_ref, k_ref, vT_ref, oT_ref, s_ref, *, tk, n_sub):
    tq = qT_ref.shape[2]
    tsub = tq // n_sub
    n_chunks = k_ref.shape[2] // tk
    subs = range(n_sub)

    def scores(c, slot):
        off = pl.multiple_of(c * tk, tk)
        k_c = k_ref[0, 0, pl.ds(off, tk), :]
        cmax = []
        for u in subs:
            s = _dot(k_c, qT_ref[0, :, u * tsub:(u + 1) * tsub])
            s_ref[slot, u] = s
            cmax.append(_col_reduce(s, jnp.maximum, jnp.max))
        return tuple(cmax)

    def softmax_pv(c, slot, carry, cmax):
        off = pl.multiple_of(c * tk, tk)
        vT_c = vT_ref[0, :, pl.ds(off, tk)]
        m_new = [jnp.maximum(carry[u][0], cmax[u]) for u in subs]
        p = [jnp.exp2(s_ref[slot, u] - m_new[u]) for u in subs]
        alpha = [jnp.exp2(carry[u][0] - m_new[u]) for u in subs]
        l = [alpha[u] * carry[u][1] + _col_reduce(p[u], jnp.add, jnp.sum) for u in subs]
        acc = [alpha[u] * carry[u][2] + _dot(vT_c, p[u].astype(BF16)) for u in subs]
        return tuple((m_new[u], l[u], acc[u]) for u in subs)

    def body(i, carry):
        state, cmax0 = carry
        c0 = 2 * i
        cmax1 = scores(c0 + 1, 1)
        state = softmax_pv(c0, 0, state, cmax0)
        cmax0 = scores(c0 + 2, 0)
        return softmax_pv(c0 + 1, 1, state, cmax1), cmax0

    cmax_first = scores(0, 0)
    init = tuple((jnp.full((1, tsub), -jnp.inf, F32), jnp.zeros((1, tsub), F32), jnp.zeros((V_DIM, tsub), F32))
                 for _ in range(n_sub))
    state, cmax0 = lax.fori_loop(0, n_chunks // 2 - 1, body, (init, cmax_first))
    cmax1 = scores(n_chunks - 1, 1)
    state = softmax_pv(n_chunks - 2, 0, state, cmax0)
    fin = softmax_pv(n_chunks - 1, 1, state, cmax1)
    for u in range(n_sub):
        _, l, acc = fin[u]
        oT_ref[0, :, u * tsub:(u + 1) * tsub] = acc * (1.0 / l)


def _attention(qT, k, vT):
    B, _, S = qT.shape
    tq = TQ_ATTN
    return pl.pallas_call(
        functools.partial(_attn_kernel, tk=min(TK_ATTN, S), n_sub=TQ_ATTN // TQ_SUB),
        grid=(B, N_HEADS, S // tq),
        in_specs=[
            pl.BlockSpec((1, HEAD_PAD, tq), lambda b, h, i: (b, h, i)),
            pl.BlockSpec((1, 1, S, HEAD_PAD), lambda b, h, i: (b, h, 0, 0)),
            pl.BlockSpec((1, V_DIM, S), lambda b, h, i: (b, h, 0)),
        ],
        out_specs=pl.BlockSpec((1, V_DIM, tq), lambda b, h, i: (b, h, i)),
        out_shape=jax.ShapeDtypeStruct((B, N_HEADS * V_DIM, S), F32),
        scratch_shapes=[pltpu.VMEM((2, TQ_ATTN // TQ_SUB, min(TK_ATTN, S), TQ_SUB), F32)],
        compiler_params=_params(("parallel", "parallel", "parallel")),
        name="attention",
    )(qT, k, vT)


def _fft_a_kernel(ur_ref, ui_ref, m1_ref, y_ref):
    u = jnp.concatenate([ur_ref[0], ui_ref[0]], axis=0)
    y_ref[0] = _dot(m1_ref[...], u).astype(BF16)


def _fft_a(ur3, ui3, m1):
    B, n1, W = ur3.shape
    L = min(FFT_A_LANES, W)
    blk = lambda b, j: (b, 0, j)
    return pl.pallas_call(
        _fft_a_kernel, grid=(B, W // L),
        in_specs=[pl.BlockSpec((1, n1, L), blk), pl.BlockSpec((1, n1, L), blk), _const_spec((2 * n1, 2 * n1))],
        out_specs=pl.BlockSpec((1, 2 * n1, L), blk),
        out_shape=jax.ShapeDtypeStruct((B, 2 * n1, W), BF16),
        compiler_params=_params(("parallel", "parallel")), name="fft_a",
    )(ur3, ui3, m1)


def _fft_b_kernel(y_ref, g_ref, o_ref, *, nk, scale):
    for j in range(nk):
        ycat = jnp.concatenate([y_ref[0, 0, j], y_ref[0, 1, j]], axis=0)
        o_ref[0, :, j * F_DIM:(j + 1) * F_DIM] = (_dot(g_ref[j], ycat) * scale).astype(BF16)


def _fft_b(y5, gtab, scale):
    B, _, n1, _, _ = y5.shape
    nk = min(FFT_B_K1, n1)
    return pl.pallas_call(
        functools.partial(_fft_b_kernel, nk=nk, scale=scale), grid=(B, n1 // nk),
        in_specs=[pl.BlockSpec((1, 2, nk, LANES, F_DIM), lambda b, j: (b, 0, j, 0, 0)),
                  pl.BlockSpec((nk, LANES, 2 * LANES), lambda b, j: (j, 0, 0))],
        out_specs=pl.BlockSpec((1, LANES, nk * F_DIM), lambda b, j: (b, 0, j)),
        out_shape=jax.ShapeDtypeStruct((B, LANES, n1 * F_DIM), BF16),
        compiler_params=_params(("parallel", "parallel")), name="fft_b",
    )(y5, gtab)


def _post_kernel(oT_ref, yf_ref, g_ref, h0_ref, wo_ref, wf_ref, wout_ref, bout_ref, lg_ref, lb_ref, h1_ref, h1T_ref):
    o = oT_ref[0].T.astype(BF16)
    y_a = _dot(o, wo_ref[...])
    y_f = _dot(yf_ref[...], wf_ref[...])
    g = g_ref[...].astype(F32)
    m = (g[:, :D_MODEL] * y_a + g[:, D_MODEL:] * y_f).astype(BF16)
    mix = _dot(m, wout_ref[...]) + bout_ref[...]
    h1 = _layer_norm(DN_ALPHA * h0_ref[...] + mix, lg_ref[...], lb_ref[...])
    h1_ref[...] = h1
    h1T_ref[...] = h1.T.astype(BF16)


def _post(oT, yf, g, h0, B, S, w):
    T = B * S
    tm = TM_POST
    nb = S // tm
    tok = lambda b, i: (b * nb + i, 0)
    return pl.pallas_call(
        _post_kernel, grid=(B, nb),
        in_specs=[pl.BlockSpec((1, N_HEADS * V_DIM, tm), lambda b, i: (b, 0, i)),
                  pl.BlockSpec((tm, F_DIM), tok), pl.BlockSpec((tm, 2 * D_MODEL), tok), pl.BlockSpec((tm, D_MODEL), tok),
                  _const_spec((N_HEADS * V_DIM, D_MODEL)), _const_spec((F_DIM, D_MODEL)), _const_spec((D_MODEL, D_MODEL)),
                  _const_spec((1, D_MODEL)), _const_spec((1, D_MODEL)), _const_spec((1, D_MODEL))],
        out_specs=[pl.BlockSpec((tm, D_MODEL), tok), pl.BlockSpec((D_MODEL, tm), lambda b, i: (0, b * nb + i))],
        out_shape=[jax.ShapeDtypeStruct((T, D_MODEL), F32), jax.ShapeDtypeStruct((D_MODEL, T), BF16)],
        compiler_params=_params(("parallel", "parallel")), name="post",
    )(oT, yf, g, h0, w["wo"], w["wf"], w["wout"], w["bout"], w["ln1_g"], w["ln1_b"])


def _extract_top16(s):
    row = lax.broadcasted_iota(jnp.int32, s.shape, 0).astype(F32)
    slot = lax.broadcasted_iota(jnp.int32, (PEER_TOPK, s.shape[1]), 0)
    rank = jnp.full(s.shape, float(PEER_TOPK), F32)
    vals = jnp.zeros((PEER_TOPK, s.shape[1]), F32)
    for r in range(PEER_TOPK):
        m = jnp.max(s, axis=0, keepdims=True)
        idx = jnp.min(jnp.where(s == m, row, float(PEER_NKEYS)), axis=0, keepdims=True)
        hit = row == idx
        s = jnp.where(hit, -jnp.inf, s)
        rank = jnp.where(hit, float(r), rank)
        vals = jnp.where(slot == r, m, vals)
    return vals, rank


def _cand_rows():
    groups = [[(i, j0 + r) if r < valid else None for r in range(SUBLANES)] for (i, j0, valid) in _CAND_GROUPS]
    groups.append([(SUBLANES + r, 0) for r in range(SUBLANES)])
    return groups


def _static_beats(cp, c):
    if cp == c:
        return 0
    if cp[0] <= c[0] and cp[1] <= c[1]:
        return 1
    if cp[0] >= c[0] and cp[1] >= c[1]:
        return 0
    return None


def _static_counts():
    rows = _cand_rows()
    out = [[0.0] * SUBLANES for _ in rows]
    for cg in rows:
        for cp in cg:
            if cp is None:
                continue
            for g, tgt in enumerate(rows):
                res = [_static_beats(cp, c) if c is not None else 0 for c in tgt]
                if all(v is not None for v in res):
                    for r in range(SUBLANES):
                        out[g][r] += float(res[r])
    return out


def _select_pairs(a_rep, b_lo, b_hi, a_hi, static_counts):
    L = b_lo.shape[1]
    sub = lax.broadcasted_iota(jnp.int32, (SUBLANES, L), 0)
    rows = _cand_rows()
    groups = []
    for (i, j0, valid) in _CAND_GROUPS:
        v = a_rep[i] + (b_lo if j0 == 0 else b_hi)
        if valid < SUBLANES:
            v = jnp.where(sub < valid, v, -jnp.inf)
        groups.append(v)
    groups.append(a_hi + jnp.broadcast_to(b_lo[0:1, :], (SUBLANES, L)))
    n_groups = len(groups)
    counts = [static_counts[g] for g in range(n_groups)]
    for gp in range(n_groups):
        for rp in range(SUBLANES):
            cp = rows[gp][rp]
            if cp is None:
                continue
            vb = jnp.broadcast_to(groups[gp][rp:rp + 1, :], (SUBLANES, L))
            for g in range(n_groups):
                if all(c is None or _static_beats(cp, c) is not None for c in rows[g]):
                    continue
                if g < gp:
                    beats = jnp.where(vb > groups[g], 1.0, 0.0)
                elif g > gp:
                    beats = jnp.where(vb >= groups[g], 1.0, 0.0)
                else:
                    beats = jnp.where(sub > rp, jnp.where(vb >= groups[g], 1.0, 0.0), jnp.where(vb > groups[g], 1.0, 0.0))
                counts[g] = counts[g] + beats
    top = groups[0][0:1, :]
    valid_rows = [g[2] for g in _CAND_GROUPS] + [SUBLANES]
    sel = []
    z = jnp.zeros((1, L), F32)
    for g in range(n_groups):
        s_g = jnp.where(counts[g] < float(PEER_TOPK), 1.0, 0.0)
        if valid_rows[g] < SUBLANES:
            s_g = jnp.where(sub < valid_rows[g], s_g, 0.0)
        sel.append(s_g)
        z = z + jnp.sum(s_g * jnp.exp(groups[g] - top), axis=0, keepdims=True)
    n = [jnp.sum(sel[0] + sel[1], axis=0, keepdims=True)]
    for g in range(2, n_groups - 1):
        n.append(jnp.sum(sel[g], axis=0, keepdims=True))
    for r in range(SUBLANES):
        n.append(sel[n_groups - 1][r:r + 1, :])
    return n, z


def _rows_from_rep(rep, lo):
    sub = lax.broadcasted_iota(jnp.int32, rep[0].shape, 0)
    out = rep[lo]
    for r in range(1, SUBLANES):
        out = jnp.where(sub == r, rep[lo + r], out)
    return out


def _oddeven_merge_pairs(n):
    pairs = []
    t = n.bit_length() - 1
    for pi in range(t):
        p = 1 << pi
        for ki in range(pi, -1, -1):
            k = 1 << ki
            for j in range(k % p, n - k, 2 * k):
                for i in range(min(k, n - j - k)):
                    if (i + j) // (2 * p) == (i + j + k) // (2 * p):
                        pairs.append((i + j, i + j + k))
    return pairs


_SORT16 = _oddeven_merge_pairs(PEER_TOPK)


def _top16_values(s):
    w = [s[g * SUBLANES:(g + 1) * SUBLANES] for g in range(PEER_NKEYS // SUBLANES)]
    for (i, j) in _SORT16:
        w[i], w[j] = jnp.maximum(w[i], w[j]), jnp.minimum(w[i], w[j])
    for shift in (4, 2, 1):
        other = [pltpu.roll(x, shift, axis=0) for x in w]
        w = [jnp.maximum(w[i], other[PEER_TOPK - 1 - i]) for i in range(PEER_TOPK)]
        d = PEER_TOPK // 2
        while d >= 1:
            for i in range(PEER_TOPK):
                if i & d == 0:
                    w[i], w[i + d] = jnp.maximum(w[i], w[i + d]), jnp.minimum(w[i], w[i + d])
            d //= 2
    return w


def _tie_flags(s, w):
    flag = jnp.zeros_like(w[0])
    for r in range(PEER_TOPK - 1):
        flag = flag + jnp.where(w[r] == w[r + 1], 1.0, 0.0)
    parts = [jnp.where(s[g * SUBLANES:(g + 1) * SUBLANES] >= w[PEER_TOPK - 1], 1.0, 0.0)
             for g in range(PEER_NKEYS // SUBLANES)]
    while len(parts) > 1:
        parts = [parts[i] + parts[i + 1] for i in range(0, len(parts), 2)]
    count = jnp.sum(parts[0], axis=0, keepdims=True)
    return flag + (count - float(PEER_TOPK))


def _peer_topk_kernel(h1T_ref, wqT_ref, keys_ref, sc_ref, n_ref, c_ref, r2_ref, e2_ref, qp_ref):
    tm = h1T_ref.shape[1]
    qp_ref[...] = _dot(wqT_ref[...], h1T_ref[...]).astype(BF16)

    def scores(h, lo):
        r0, r1 = 2 * h * PEER_HALF, (2 * h + 1) * PEER_HALF
        if not isinstance(h, int):
            r0, r1 = pl.multiple_of(r0, PEER_HALF), pl.multiple_of(r1, PEER_HALF)
        s1 = _dot(keys_ref[2 * h], qp_ref[pl.ds(r0, PEER_HALF), pl.ds(lo, LANES)])
        s2 = _dot(keys_ref[2 * h + 1], qp_ref[pl.ds(r1, PEER_HALF), pl.ds(lo, LANES)])
        return s1, s2

    def store(h, lo, n_a, c_a, rank2, e2):
        n_ref[h, :, pl.ds(lo, LANES)] = n_a
        c_ref[h, :, pl.ds(lo, LANES)] = c_a
        r2_ref[h, :, pl.ds(lo, LANES)] = rank2.astype(BF16)
        e2_ref[h, :, pl.ds(lo, LANES)] = e2.astype(BF16)

    def fast_head(h, lo):
        s1, s2 = scores(h, lo)
        w1, w2 = _top16_values(s1), _top16_values(s2)
        n, z = _select_pairs(w1, _rows_from_rep(w2, 0), _rows_from_rep(w2, SUBLANES), _rows_from_rep(w1, SUBLANES),
                             sc_ref)
        n_parts, r_parts = [], []
        for g in range(PEER_NKEYS // SUBLANES):
            s1_g, s2_g = s1[g * SUBLANES:(g + 1) * SUBLANES], s2[g * SUBLANES:(g + 1) * SUBLANES]
            n_g = jnp.zeros_like(s1_g)
            r_g = jnp.full_like(s2_g, float(PEER_TOPK))
            for r in range(PEER_TOPK):
                n_g = jnp.where(s1_g == w1[r], n[r], n_g)
                r_g = jnp.where(s2_g == w2[r], float(r), r_g)
            n_parts.append(n_g)
            r_parts.append(r_g)
        c_a = jnp.exp(s1 - w1[0][0:1, :]) * (GELU_HALF / z)
        e2 = jnp.exp(s2 - w2[0][0:1, :])
        store(h, lo, jnp.concatenate(n_parts, axis=0), c_a, jnp.concatenate(r_parts, axis=0), e2)
        return _tie_flags(s1, w1) + _tie_flags(s2, w2)

    def exact_head(h, lo):
        s1, s2 = scores(h, lo)
        a_vals, rank1 = _extract_top16(s1)
        b_vals, rank2 = _extract_top16(s2)
        a_rep = [jnp.broadcast_to(a_vals[i:i + 1, :], (SUBLANES, LANES)) for i in range(PEER_TOPK)]
        n, z = _select_pairs(a_rep, b_vals[:SUBLANES], b_vals[SUBLANES:], a_vals[SUBLANES:], sc_ref)
        n_a = jnp.zeros_like(s1)
        for i in range(PEER_TOPK):
            n_a = n_a + jnp.where(rank1 == float(i), n[i], 0.0)
        c_a = jnp.exp(s1 - a_vals[0:1, :]) * (GELU_HALF / z)
        e2 = jnp.exp(s2 - b_vals[0:1, :])
        store(h, lo, n_a, c_a, rank2, e2)

    def chunk(ci, _):
        lo = pl.multiple_of(ci * LANES, LANES)
        flags = jnp.zeros((SUBLANES, LANES), F32)
        for h in range(PEER_HEADS):
            flags = flags + fast_head(h, lo)

        @pl.when(jnp.max(flags) > 0.0)
        def _():
            def per_head(h, carry):
                exact_head(h, lo)
                return carry
            lax.fori_loop(0, PEER_HEADS, per_head, 0)
        return 0

    lax.fori_loop(0, tm // LANES, chunk, 0)


def _peer_topk(h1T, wqT, keys):
    T = h1T.shape[1]
    tm = TM_TOPK
    nq = wqT.shape[0]
    col = lambda i: (0, i)
    col3 = lambda i: (0, 0, i)
    static = jnp.broadcast_to(jnp.asarray(_static_counts(), F32)[:, :, None], (len(_CAND_GROUPS) + 1, SUBLANES, LANES))
    return pl.pallas_call(
        _peer_topk_kernel, grid=(T // tm,),
        in_specs=[pl.BlockSpec((D_MODEL, tm), col), _const_spec((nq, D_MODEL)),
                  _const_spec((2 * PEER_HEADS, PEER_NKEYS, PEER_HALF)), _const_spec(static.shape)],
        out_specs=[pl.BlockSpec((PEER_HEADS, PEER_NKEYS, tm), col3), pl.BlockSpec((PEER_HEADS, PEER_NKEYS, tm), col3),
                   pl.BlockSpec((PEER_HEADS, PEER_NKEYS, tm), col3), pl.BlockSpec((PEER_HEADS, PEER_NKEYS, tm), col3)],
        out_shape=[jax.ShapeDtypeStruct((PEER_HEADS, PEER_NKEYS, T), F32),
                   jax.ShapeDtypeStruct((PEER_HEADS, PEER_NKEYS, T), F32),
                   jax.ShapeDtypeStruct((PEER_HEADS, PEER_NKEYS, T), BF16),
                   jax.ShapeDtypeStruct((PEER_HEADS, PEER_NKEYS, T), BF16)],
        scratch_shapes=[pltpu.VMEM((nq, tm), BF16)],
        compiler_params=_params(("parallel",)), name="peer_topk",
    )(h1T, wqT, keys, static)


def _peer_dense_kernel(xT_ref, wu_ref, wvT_ref, n_ref, c_ref, r2_ref, e2_ref, h1_ref, lg_ref, lb_ref, out_ref,
                       acc_ref, *, eb):
    j = pl.program_id(1)
    tm = xT_ref.shape[1]

    @pl.when(j == 0)
    def _():
        acc_ref[...] = jnp.zeros_like(acc_ref)

    n_chains = eb // PEER_CHAIN
    rows = [slice(ch * PEER_CHAIN, (ch + 1) * PEER_CHAIN) for ch in range(n_chains)]

    def first_matmul(ch):
        return _dot(wu_ref[rows[ch], :], xT_ref[...])

    def gate_gelu(ch, hpre):
        acts = []
        for aa in range(PEER_CHAIN // PEER_NKEYS):
            a_key = (j * n_chains + ch) * (PEER_CHAIN // PEER_NKEYS) + aa
            gate = jnp.zeros((PEER_NKEYS, tm), BF16)
            for h in range(PEER_HEADS):
                n_b = jnp.broadcast_to(n_ref[h, pl.ds(a_key, 1), :], (PEER_NKEYS, tm)).astype(BF16)
                c_b = jnp.broadcast_to(c_ref[h, pl.ds(a_key, 1), :], (PEER_NKEYS, tm)).astype(BF16)
                gate = gate + jnp.where(r2_ref[h] < n_b, e2_ref[h], jnp.zeros((), BF16)) * c_b
            hp = hpre[aa * PEER_NKEYS:(aa + 1) * PEER_NKEYS, :]
            act = hp * (1.0 + lax.erf(hp * (1.0 / math.sqrt(2.0))))
            acts.append(act.astype(BF16) * gate)
        return jnp.concatenate(acts, axis=0)

    hpre, act = {}, {}
    for t in range(n_chains + 2):
        if t < n_chains:
            hpre[t] = first_matmul(t)
        if 0 <= t - 1 < n_chains:
            act[t - 1] = gate_gelu(t - 1, hpre.pop(t - 1))
        if 0 <= t - 2 < n_chains:
            acc_ref[...] += _dot(wvT_ref[:, rows[t - 2]], act.pop(t - 2))

    @pl.when(j == pl.num_programs(1) - 1)
    def _():
        y = acc_ref[...].T
        out_ref[...] = _layer_norm(DN_ALPHA * h1_ref[...] + y, lg_ref[...], lb_ref[...])


def _peer_dense(h1T, wu, wvT, n_tab, c_tab, r2, e2, h1, lg, lb):
    T = h1T.shape[1]
    tm, eb = TM_PEER, EB_PEER
    col = lambda i, j: (0, i)
    col3 = lambda i, j: (0, 0, i)
    row = lambda i, j: (i, 0)
    return pl.pallas_call(
        functools.partial(_peer_dense_kernel, eb=eb), grid=(T // tm, PEER_EXPERTS // eb),
        in_specs=[pl.BlockSpec((D_MODEL, tm), col),
                  pl.BlockSpec((eb, D_MODEL), lambda i, j: (j, 0)),
                  pl.BlockSpec((D_MODEL, eb), lambda i, j: (0, j)),
                  pl.BlockSpec((PEER_HEADS, PEER_NKEYS, tm), col3), pl.BlockSpec((PEER_HEADS, PEER_NKEYS, tm), col3),
                  pl.BlockSpec((PEER_HEADS, PEER_NKEYS, tm), col3), pl.BlockSpec((PEER_HEADS, PEER_NKEYS, tm), col3),
                  pl.BlockSpec((tm, D_MODEL), row), _const_spec((1, D_MODEL)), _const_spec((1, D_MODEL))],
        out_specs=pl.BlockSpec((tm, D_MODEL), row),
        out_shape=jax.ShapeDtypeStruct((T, D_MODEL), F32),
        scratch_shapes=[pltpu.VMEM((D_MODEL, tm), F32)],
        compiler_params=_params(("parallel", "arbitrary")), name="peer_dense",
    )(h1T, wu, wvT, n_tab, c_tab, r2, e2, h1, lg, lb)


def _transpose_cast_kernel(w_ref, o_ref):
    o_ref[...] = w_ref[...].T.astype(BF16)


def _transpose_cast(w):
    rows, d = w.shape
    tr = TM_WPREP
    return pl.pallas_call(
        _transpose_cast_kernel, grid=(rows // tr,),
        in_specs=[pl.BlockSpec((tr, d), lambda i: (i, 0))],
        out_specs=pl.BlockSpec((d, tr), lambda i: (0, i)),
        out_shape=jax.ShapeDtypeStruct((d, rows), BF16),
        compiler_params=_params(("parallel",)), name="wv_prep",
    )(w)


def _pad_cols(w, lo, total):
    return jnp.pad(w, ((0, 0), (lo, total - lo - w.shape[1])))


def _prepare(S, w_in, b_in, q_norm_g, kv_norm_g, w_uq, w_uk, w_uv, w_o_attn, w_fourier, w_out, b_out, ln1_g, ln1_b,
             peer_w_q, peer_keys, peer_w_u, peer_w_v):
    o_kv, o_kr, o_z, o_g = Q_LORA, Q_LORA + KV_LORA, Q_LORA + KV_LORA + QK_ROPE_DIM, Q_LORA + KV_LORA + QK_ROPE_DIM + F_DIM
    row = lambda v: v.reshape(1, -1).astype(F32)
    w = {}
    w["wq"], w["bq"], w["gq"] = w_in[:, :o_kv].astype(BF16), row(b_in[:o_kv]), row(q_norm_g)
    w["wkv"], w["bkv"], w["gkv"] = w_in[:, o_kv:o_kr].astype(BF16), row(b_in[o_kv:o_kr]), row(kv_norm_g)
    wkr, bkr = w_in[:, o_kr:o_z], b_in[o_kr:o_z].reshape(1, -1)
    swap = lambda m: jnp.concatenate([m[:, ROPE_HALF:], m[:, :ROPE_HALF]], axis=1)
    w["wkr"] = jnp.concatenate([_pad_cols(wkr, QK_NOPE_DIM, HEAD_PAD), _pad_cols(swap(wkr), QK_NOPE_DIM, HEAD_PAD)],
                               axis=1).astype(BF16)
    w["bkr"] = jnp.concatenate([_pad_cols(bkr, QK_NOPE_DIM, HEAD_PAD), _pad_cols(swap(bkr), QK_NOPE_DIM, HEAD_PAD)],
                               axis=1).astype(F32)
    w["wz"], w["bz"] = w_in[:, o_z:o_g].astype(BF16), row(b_in[o_z:o_g])
    w["wg"], w["bg"] = w_in[:, o_g:].astype(BF16), row(b_in[o_g:])
    wq3 = w_uq.reshape(Q_LORA, N_HEADS, QK_DIM).transpose(1, 0, 2)
    rope3 = wq3[:, :, QK_NOPE_DIM:]
    rope3_sw = jnp.concatenate([rope3[:, :, ROPE_HALF:], rope3[:, :, :ROPE_HALF]], axis=2)
    padh = lambda m: jnp.pad(m, ((0, 0), (0, 0), (0, HEAD_PAD - m.shape[2])))
    pair = lambda m: m.reshape(N_HEADS // 2, 2, m.shape[1], HEAD_PAD).transpose(0, 2, 1, 3).reshape(
        N_HEADS // 2, m.shape[1], 2 * HEAD_PAD)
    w["wqu"] = pair(padh(wq3)).astype(BF16)
    w["wqr"] = pair(padh(jnp.concatenate([jnp.zeros_like(wq3[:, :, :QK_NOPE_DIM]), rope3_sw], axis=2))).astype(BF16)
    wk3 = w_uk.reshape(KV_LORA, N_HEADS, QK_NOPE_DIM).transpose(1, 0, 2)
    w["wku"] = pair(padh(wk3)).astype(BF16)
    w["wvu"] = w_uv.astype(BF16)
    pos = jnp.arange(S, dtype=F32)
    inv = 1.0 / (ROPE_THETA ** (jnp.arange(0, QK_ROPE_DIM, 2, dtype=F32) / QK_ROPE_DIM))
    ang = pos[:, None] * inv[None, :]
    cos, sin = jnp.cos(ang), jnp.sin(ang)
    ones, zeros = jnp.ones((S, QK_NOPE_DIM), F32), jnp.zeros((S, QK_NOPE_DIM), F32)
    tail = jnp.zeros((S, HEAD_PAD - QK_DIM), F32)
    scale = QK_DIM ** -0.5 * math.log2(math.e)
    w["cos_q"] = jnp.concatenate([ones, cos, cos, tail], axis=1) * scale
    w["sin_q"] = jnp.concatenate([zeros, -sin, sin, tail], axis=1) * scale
    w["cos_k"] = jnp.concatenate([zeros, cos, cos, tail], axis=1)
    w["sin_k"] = jnp.concatenate([zeros, -sin, sin, tail], axis=1)
    n1 = S // LANES
    idx = lambda n: jnp.arange(n, dtype=jnp.int32)
    ang_c = (2.0 * math.pi / F_GROUP_DIM) * ((idx(F_GROUP_DIM)[:, None] * idx(F_GROUP_DIM)[None, :]) % F_GROUP_DIM).astype(F32)
    w["cs128"] = jnp.concatenate([jnp.cos(ang_c), jnp.sin(ang_c)], axis=1).astype(BF16)
    ang_1 = (2.0 * math.pi / n1) * ((idx(n1)[:, None] * idx(n1)[None, :]) % n1).astype(F32)
    c1, s1 = jnp.cos(ang_1), jnp.sin(ang_1)
    w["m1"] = jnp.concatenate([jnp.concatenate([c1, s1], axis=1), jnp.concatenate([-s1, c1], axis=1)], axis=0).astype(BF16)
    kk = idx(n1)[:, None, None] + n1 * idx(LANES)[None, :, None]
    ang_g = (2.0 * math.pi / S) * ((kk * idx(LANES)[None, None, :]) % S).astype(F32)
    w["gtab"] = jnp.concatenate([jnp.cos(ang_g), jnp.sin(ang_g)], axis=2).astype(BF16)
    w["wo"], w["wf"], w["wout"] = w_o_attn.astype(BF16), w_fourier.astype(BF16), w_out.astype(BF16)
    w["bout"], w["ln1_g"], w["ln1_b"] = row(b_out), row(ln1_g), row(ln1_b)
    w["wqT"] = peer_w_q.astype(BF16).T
    w["keys"] = peer_keys.reshape(2 * PEER_HEADS, PEER_NKEYS, PEER_HALF).astype(BF16)
    w["wu"] = peer_w_u.astype(BF16)
    w["wvT"] = _transpose_cast(peer_w_v)
    return w


def kernel(x, ln0_g, ln0_b, w_in, b_in, q_norm_g, kv_norm_g, w_uq, w_uk, w_uv, w_o_attn, w_fourier, w_out, b_out, ln1_g,
           ln1_b, peer_w_q, peer_keys, peer_w_u, peer_w_v, ln2_g, ln2_b):
    B, S, D = x.shape
    assert D == D_MODEL and w_in.shape[0] == DEPTH
    T = B * S
    n1 = S // LANES
    assert S % max(TM_FRONT, TQ_ATTN, TM_POST, LANES * SUBLANES) == 0 and T % max(TM_PEER, TM_TOPK) == 0
    row = lambda v: v.reshape(1, -1).astype(F32)
    h = x.reshape(T, D)
    w = _prepare(S, w_in[0], b_in[0], q_norm_g[0], kv_norm_g[0], w_uq[0], w_uk[0], w_uv[0], w_o_attn[0], w_fourier[0],
                 w_out[0], b_out[0], ln1_g[0], ln1_b[0], peer_w_q[0], peer_keys[0], peer_w_u[0], peer_w_v[0])
    h0, qT, k, vT, ur, ui, g = _front(h, B, S, row(ln0_g), row(ln0_b), w)
    oT = _attention(qT, k, vT)
    y = _fft_a(ur.reshape(B, n1, LANES * F_DIM), ui.reshape(B, n1, LANES * F_DIM), w["m1"])
    yf = _fft_b(y.reshape(B, 2, n1, LANES, F_DIM), w["gtab"], 1.0 / math.sqrt(S * F_GROUP_DIM))
    h1, h1T = _post(oT, yf.reshape(T, F_DIM), g, h0, B, S, w)
    n_tab, c_tab, r2, e2 = _peer_topk(h1T, w["wqT"], w["keys"])
    out = _peer_dense(h1T, w["wu"], w["wvT"], n_tab, c_tab, r2, e2, h1, row(ln2_g[0]), row(ln2_b[0]))
    return out.reshape(B, S, D)
```
